```python
import jax, jax.numpy as jnp
from jax import lax
import numpy as np

D_MODEL = 1024
BATCH = 8
SEQ = 4096
DEPTH = 2

HEAD_DIM = 64
N_Q_HEADS = D_MODEL // HEAD_DIM
N_KV_HEADS = 4
GQA_GROUP = N_Q_HEADS // N_KV_HEADS
ATT_WIDTH = N_Q_HEADS * HEAD_DIM
KV_WIDTH = N_KV_HEADS * HEAD_DIM
ATT_IN = 2 * ATT_WIDTH + 2 * KV_WIDTH
WINDOW = 128
BLOCK = 128
ROPE_THETA = 10000.0
MASK_VALUE = -1e30

RWKV_HEAD = 64
RWKV_HEADS = D_MODEL // RWKV_HEAD
RWKV_WIDTH = RWKV_HEADS * RWKV_HEAD
DECAY_LORA = 64
AAA_LORA = 64
N_LERP = 6
GN_EPS = 64e-5

PLE_DIM = 256
NORM_EPS = 1e-6
N_ATTN_LAYERS = (DEPTH + 1) // 2
N_RWKV_LAYERS = DEPTH // 2

kernel_name = 'hybrid_swa_sink_rwkv7_ple'


def rms_norm(x, g):
    xf = x.astype(jnp.float32)
    y = xf * lax.rsqrt(jnp.mean(xf * xf, axis=-1, keepdims=True) + NORM_EPS)
    return (y * g.astype(jnp.float32)).astype(x.dtype)


def rope(x, pos):
    half = HEAD_DIM // 2
    inv = ROPE_THETA ** (-jnp.arange(half, dtype=jnp.float32) / half)
    ang = pos.astype(jnp.float32)[:, None] * inv[None, :]
    cos = jnp.cos(ang)[None, :, None, :]
    sin = jnp.sin(ang)[None, :, None, :]
    xf = x.astype(jnp.float32)
    x1, x2 = xf[..., :half], xf[..., half:]
    out = jnp.concatenate([x1 * cos - x2 * sin, x2 * cos + x1 * sin], axis=-1)
    return out.astype(x.dtype)


def swa_sink_mixer(h, w_in, b_in, sinks, w_out):
    B, T, _ = h.shape
    nb = T // BLOCK
    proj = h @ w_in + b_in
    q, k, v, z = jnp.split(proj, [ATT_WIDTH, ATT_WIDTH + KV_WIDTH, ATT_WIDTH + 2 * KV_WIDTH], axis=-1)
    pos = jnp.arange(T)
    q = rope(q.reshape(B, T, N_Q_HEADS, HEAD_DIM), pos)
    k = rope(k.reshape(B, T, N_KV_HEADS, HEAD_DIM), pos)
    v = v.reshape(B, T, N_KV_HEADS, HEAD_DIM)
    qb = q.reshape(B, nb, BLOCK, N_KV_HEADS, GQA_GROUP, HEAD_DIM)
    kb = k.reshape(B, nb, BLOCK, N_KV_HEADS, HEAD_DIM)
    vb = v.reshape(B, nb, BLOCK, N_KV_HEADS, HEAD_DIM)
    pad = ((0, 0), (1, 0), (0, 0), (0, 0), (0, 0))
    kw = jnp.concatenate([jnp.pad(kb, pad)[:, :-1], kb], axis=2)
    vw = jnp.concatenate([jnp.pad(vb, pad)[:, :-1], vb], axis=2)
    s = jnp.einsum('bnqkgd,bnskd->bnkgqs', qb, kw).astype(jnp.float32) * (HEAD_DIM ** -0.5)
    qi = jnp.arange(BLOCK)[:, None] + BLOCK
    si = jnp.arange(2 * BLOCK)[None, :]
    diff = qi - si
    band = (diff >= 0) & (diff < WINDOW)
    has_prev = (jnp.arange(nb)[:, None] > 0) | (jnp.arange(2 * BLOCK)[None, :] >= BLOCK)
    mask = band[None, :, :] & has_prev[:, None, :]
    s = jnp.where(mask[None, :, None, None, :, :], s, MASK_VALUE)
    sink = jnp.broadcast_to(sinks.astype(jnp.float32).reshape(N_KV_HEADS, GQA_GROUP)[None, None, :, :, None, None],
                            s.shape[:-1] + (1,))
    probs = jax.nn.softmax(jnp.concatenate([s, sink], axis=-1), axis=-1)[..., :-1]
    o = jnp.einsum('bnkgqs,bnskd->bnqkgd', probs.astype(vw.dtype), vw).reshape(B, T, ATT_WIDTH)
    return (o * jax.nn.silu(z)) @ w_out


def rwkv7_mixer(h, mu, w_in, w0, w1, w2, a0, a1, a2, k_k, k_a, r_k, gn_g, gn_b, w_out):
    B, T, C = h.shape
    H, N = RWKV_HEADS, RWKV_HEAD
    xx = jnp.pad(h, ((0, 0), (1, 0), (0, 0)))[:, :-1] - h
    xs = h[None] + xx[None] * mu[:, None, None, :]
    proj = jnp.einsum('cbtd,dce->cbte', xs[:4], w_in.reshape(C, 4, RWKV_WIDTH))
    r, k, v, z = proj[0], proj[1], proj[2], proj[3]
    xw, xa = xs[4], xs[5]
    w = -jax.nn.softplus(-(w0 + jnp.tanh(xw @ w1) @ w2)) - 0.5
    decay = jnp.exp(-jnp.exp(w.astype(jnp.float32)))
    a = jax.nn.sigmoid(a0 + (xa @ a1) @ a2)
    r4 = r.reshape(B, T, H, N).astype(jnp.float32)
    k4 = k.reshape(B, T, H, N).astype(jnp.float32)
    v4 = v.reshape(B, T, H, N).astype(jnp.float32)
    a4 = a.reshape(B, T, H, N).astype(jnp.float32)
    d4 = decay.reshape(B, T, H, N)
    kk = k4 * k_k.reshape(H, N).astype(jnp.float32)
    kk = kk / jnp.maximum(jnp.sqrt(jnp.sum(kk * kk, axis=-1, keepdims=True)), 1e-12)
    k4 = k4 * (1.0 + (a4 - 1.0) * k_a.reshape(H, N).astype(jnp.float32))

    def step(S, inp):
        r_t, d_t, k_t, v_t, kk_t, a_t = inp
        sa = jnp.einsum('bhvk,bhk->bhv', S, -kk_t)
        S = S * d_t[:, :, None, :] + sa[..., None] * (kk_t * a_t)[:, :, None, :] + v_t[..., None] * k_t[:, :, None, :]
        y = jnp.einsum('bhvk,bhk->bhv', S, r_t)
        return S, y

    seq_in = tuple(jnp.swapaxes(t, 0, 1) for t in (r4, d4, k4, v4, kk, a4))
    S0 = jnp.zeros((B, H, N, N), jnp.float32)
    _, ys = lax.scan(step, S0, seq_in)
    y = jnp.swapaxes(ys, 0, 1)
    mean = jnp.mean(y, axis=-1, keepdims=True)
    var = jnp.mean(jnp.square(y - mean), axis=-1, keepdims=True)
    y = ((y - mean) * lax.rsqrt(var + GN_EPS)).reshape(B, T, C) * gn_g.astype(jnp.float32) + gn_b.astype(jnp.float32)
    bonus = jnp.sum(r4 * k4 * r_k.astype(jnp.float32), axis=-1, keepdims=True) * v4
    y = (y + bonus.reshape(B, T, C)).astype(h.dtype)
    return (y * jax.nn.silu(z)) @ w_out


def setup_inputs(seed: int = 0) -> dict:
    key = jax.random.key(seed)
    ks = jax.random.split(key, 32)
    f = jnp.float32
    nA, nR = N_ATTN_LAYERS, N_RWKV_LAYERS
    C = D_MODEL

    def nrm(k, shape, scale):
        return jax.random.normal(k, shape, f) * scale

    return {
        'x': jax.random.normal(ks[0], (BATCH, SEQ, C), f),
        'p': jax.random.normal(ks[1], (DEPTH, BATCH, SEQ, PLE_DIM), f),
        'norm_g': 1.0 + nrm(ks[2], (DEPTH, C), 0.02),
        'attn_w_in': nrm(ks[3], (nA, C, ATT_IN), C ** -0.5),
        'attn_b_in': nrm(ks[4], (nA, ATT_IN), 0.02),
        'attn_sinks': nrm(ks[5], (nA, N_Q_HEADS), 0.5),
        'attn_w_out': nrm(ks[6], (nA, ATT_WIDTH, C), ATT_WIDTH ** -0.5),
        'rwkv_mu': jax.random.uniform(ks[7], (nR, N_LERP, C), f),
        'rwkv_w_in': nrm(ks[8], (nR, C, 4 * RWKV_WIDTH), C ** -0.5),
        'rwkv_w0': -2.0 + nrm(ks[9], (nR, RWKV_WIDTH), 1.0),
        'rwkv_w1': nrm(ks[10], (nR, C, DECAY_LORA), C ** -0.5),
        'rwkv_w2': nrm(ks[11], (nR, DECAY_LORA, RWKV_WIDTH), 0.1 * DECAY_LORA ** -0.5),
        'rwkv_a0': nrm(ks[12], (nR, RWKV_WIDTH), 0.1),
        'rwkv_a1': nrm(ks[13], (nR, C, AAA_LORA), C ** -0.5),
        'rwkv_a2': nrm(ks[14], (nR, AAA_LORA, RWKV_WIDTH), AAA_LORA ** -0.5),
        'rwkv_k_k': 0.85 + nrm(ks[15], (nR, RWKV_WIDTH), 0.05),
        'rwkv_k_a': 1.0 + nrm(ks[16], (nR, RWKV_WIDTH), 0.05),
        'rwkv_r_k': nrm(ks[17], (nR, RWKV_HEADS, RWKV_HEAD), 0.1),
        'rwkv_gn_g': 1.0 + nrm(ks[18], (nR, RWKV_WIDTH), 0.02),
        'rwkv_gn_b': nrm(ks[19], (nR, RWKV_WIDTH), 0.02),
        'rwkv_w_out': nrm(ks[20], (nR, RWKV_WIDTH, C), RWKV_WIDTH ** -0.5),
        'ple_w_proj': nrm(ks[21], (DEPTH, PLE_DIM, C), PLE_DIM ** -0.5),
        'ple_w_gate': nrm(ks[22], (DEPTH, C, C), C ** -0.5),
        'final_norm_g': 1.0 + nrm(ks[23], (C,), 0.02),
    }


def reference(x, p, norm_g, attn_w_in, attn_b_in, attn_sinks, attn_w_out,
              rwkv_mu, rwkv_w_in, rwkv_w0, rwkv_w1, rwkv_w2, rwkv_a0, rwkv_a1, rwkv_a2,
              rwkv_k_k, rwkv_k_a, rwkv_r_k, rwkv_gn_g, rwkv_gn_b, rwkv_w_out,
              ple_w_proj, ple_w_gate, final_norm_g):
    h = x
    for i in range(DEPTH):
        hn = rms_norm(h, norm_g[i])
        j = i // 2
        if i % 2 == 0:
            m = swa_sink_mixer(hn, attn_w_in[j], attn_b_in[j], attn_sinks[j], attn_w_out[j])
        else:
            m = rwkv7_mixer(hn, rwkv_mu[j], rwkv_w_in[j], rwkv_w0[j], rwkv_w1[j], rwkv_w2[j],
                            rwkv_a0[j], rwkv_a1[j], rwkv_a2[j], rwkv_k_k[j], rwkv_k_a[j], rwkv_r_k[j],
                            rwkv_gn_g[j], rwkv_gn_b[j], rwkv_w_out[j])
        h = h + m
        h = h + jax.nn.sigmoid(h @ ple_w_gate[i]) * (p[i] @ ple_w_proj[i])
    return rms_norm(h, final_norm_g)
```

```python
import functools

import jax
import jax.numpy as jnp
from jax import lax
from jax.experimental import pallas as pl
from jax.experimental.pallas import tpu as pltpu

F32 = jnp.float32
BF16 = jnp.bfloat16

HEAD = 64
N_KV = 4
GQA = 4
WINDOW = 128
ROPE_THETA = 10000.0
MASK_VALUE = -1e30
NORM_EPS = 1e-6
GN_EPS = 64e-5
CHUNK = 64
GROUP = 256
VMEM_LIMIT = 56 * 1024 * 1024


def _cparams(n_axes):
    return pltpu.CompilerParams(
        dimension_semantics=("arbitrary",) * n_axes, vmem_limit_bytes=VMEM_LIMIT)


def _rms(x, g):
    ms = jnp.mean(x * x, axis=-1, keepdims=True)
    return x * lax.rsqrt(ms + NORM_EPS) * g


def _split_bf16(x, terms):
    parts = []
    rem = x
    for _ in range(terms):
        p = rem.astype(BF16)
        parts.append(p)
        rem = rem - p.astype(F32)
    return parts


def _dot(a, b):
    return jnp.dot(a, b, preferred_element_type=F32)


def _dot_nt(a, b):
    return lax.dot_general(a, b, (((1,), (1,)), ((), ())), preferred_element_type=F32)


def _dot_tn(a, b):
    return lax.dot_general(a, b, (((0,), (0,)), ((), ())), preferred_element_type=F32)


def _seg_sum(x, ones_bd):
    hi, lo = _split_bf16(x, 2)
    return _dot(hi, ones_bd) + _dot(lo, ones_bd)


def _block_ones():
    r = lax.broadcasted_iota(jnp.int32, (GROUP, GROUP), 0) >> 6
    c = lax.broadcasted_iota(jnp.int32, (GROUP, GROUP), 1) >> 6
    return r == c


def _attn_in_kernel(x_ref, g_ref, w_ref, b_ref, cos_ref, sin_ref, q_ref, k_ref, v_ref, z_ref):
    hn = _rms(x_ref[...], g_ref[...]).astype(BF16)
    cos = cos_ref[...]
    sin = sin_ref[...]
    first = (lax.broadcasted_iota(jnp.int32, cos.shape, 1) & (HEAD - 1)) < HEAD // 2

    def rope(t):
        rot = jnp.where(first, pltpu.roll(t, 128 - HEAD // 2, 1), pltpu.roll(t, HEAD // 2, 1))
        return t * cos + rot * sin

    d = q_ref.shape[1]
    kv = k_ref.shape[1]
    q = _dot(hn, w_ref[:, 0:d]) + b_ref[:, 0:d]
    for c in range(d // 128):
        sl = slice(c * 128, (c + 1) * 128)
        q_ref[:, sl] = (rope(q[:, sl]) * (HEAD ** -0.5)).astype(BF16)
    k = _dot(hn, w_ref[:, d:d + kv]) + b_ref[:, d:d + kv]
    for c in range(kv // 128):
        sl = slice(c * 128, (c + 1) * 128)
        k_ref[:, sl] = rope(k[:, sl]).astype(BF16)
    v = _dot(hn, w_ref[:, d + kv:d + 2 * kv]) + b_ref[:, d + kv:d + 2 * kv]
    v_ref[...] = v.astype(BF16)
    z_ref[...] = _dot(hn, w_ref[:, d + 2 * kv:]) + b_ref[:, d + 2 * kv:]


def _swa_kernel(sink_ref, q_ref, kc_ref, kp_ref, vc_ref, vp_ref, z_ref, o_ref):
    n = pl.program_id(1)
    qi = lax.broadcasted_iota(jnp.int32, (WINDOW, 2 * WINDOW), 0)
    si = lax.broadcasted_iota(jnp.int32, (WINDOW, 2 * WINDOW), 1)
    diff = qi + WINDOW - si
    mask = (diff >= 0) & (diff < WINDOW) & ((si >= WINDOW) | (n > 0))
    for kh in range(N_KV):
        ks = slice(kh * HEAD, (kh + 1) * HEAD)
        kcat = jnp.concatenate([kp_ref[:, ks], kc_ref[:, ks]], axis=0)
        vcat = jnp.concatenate([vp_ref[:, ks], vc_ref[:, ks]], axis=0)
        outs = []
        for g in range(GQA):
            h = kh * GQA + g
            hs = slice(h * HEAD, (h + 1) * HEAD)
            s = _dot_nt(q_ref[:, hs], kcat)
            s = jnp.where(mask, s, MASK_VALUE)
            sink = sink_ref[h]
            m = jnp.maximum(jnp.max(s, axis=-1, keepdims=True), sink)
            p = jnp.exp(s - m)
            l = jnp.sum(p, axis=-1, keepdims=True) + jnp.exp(sink - m)
            o = _dot(p.astype(BF16), vcat) / l
            z = z_ref[:, hs]
            outs.append(o * (z * jax.nn.sigmoid(z)))
        for j in range(GQA // 2):
            h = kh * GQA + 2 * j
            o_ref[:, h * HEAD:(h + 2) * HEAD] = jnp.concatenate(
                [outs[2 * j], outs[2 * j + 1]], axis=1).astype(BF16)


def _out_ple_kernel(og_ref, h_ref, p_ref, wo_ref, wg_ref, wp_ref, g_ref, *out_refs, emit_h):
    h1 = h_ref[...] + _dot(og_ref[...], wo_ref[...])
    gate = jax.nn.sigmoid(_dot(h1.astype(BF16), wg_ref[...]))
    h2 = h1 + gate * _dot(p_ref[...].astype(BF16), wp_ref[...])
    if emit_h:
        out_refs[0][...] = h2
    out_refs[-1][...] = _rms(h2, g_ref[...])


def _rwkv_in_kernel(hn_ref, prev_ref, mu_ref, w_ref, w0_ref, w1_ref, w2_ref, a0_ref, a1_ref,
                    a2_ref, kk_ref, ka_ref, r_out, k_out, v_out, z_out, kk_out, b_out, ld_out,
                    *, tiles_per_seq):
    i = pl.program_id(0)
    hn = hn_ref[...]
    tm, d = hn.shape
    prev_row = jnp.where(i % tiles_per_seq == 0, 0.0, prev_ref[7:8, :])
    row = lax.broadcasted_iota(jnp.int32, (tm, d), 0)
    shifted = jnp.where(row == 0, prev_row, pltpu.roll(hn, 1, 0))
    xx = shifted - hn

    def lerp(c):
        return (hn + xx * mu_ref[c:c + 1, :]).astype(BF16)

    r_out[...] = _dot(lerp(0), w_ref[:, 0:d])
    k = _dot(lerp(1), w_ref[:, d:2 * d])
    v_out[...] = _dot(lerp(2), w_ref[:, 2 * d:3 * d])
    z_out[...] = _dot(lerp(3), w_ref[:, 3 * d:4 * d])

    lw = jnp.tanh(_dot(lerp(4), w1_ref[...])).astype(BF16)
    u = -(w0_ref[...] + _dot(lw, w2_ref[...]))
    softplus = jnp.maximum(u, 0.0) + jnp.log(1.0 + jnp.exp(-jnp.abs(u)))
    ld_out[...] = -jnp.exp(-softplus - 0.5)

    la = _dot(lerp(5), a1_ref[...]).astype(BF16)
    a = jax.nn.sigmoid(a0_ref[...] + _dot(la, a2_ref[...]))

    ones_bd = jnp.where(_block_ones(), 1.0, 0.0).astype(BF16)
    kk = k * kk_ref[...]
    for g in range(d // GROUP):
        sl = slice(g * GROUP, (g + 1) * GROUP)
        kg = kk[:, sl]
        nrm = jnp.sqrt(_seg_sum(kg * kg, ones_bd))
        kg = kg / jnp.maximum(nrm, 1e-12)
        kk_out[:, sl] = kg
        b_out[:, sl] = kg * a[:, sl]
    k_out[...] = k * (1.0 + (a - 1.0) * ka_ref[...])


def _scan_masks():
    t = lax.broadcasted_iota(jnp.int32, (CHUNK, GROUP), 0)
    s = lax.broadcasted_iota(jnp.int32, (CHUNK, GROUP), 1) & (HEAD - 1)
    same = lambda sh: (t >> sh) == (s >> sh)
    return dict(strict=s < t, incl=s <= t, eye=s == t, b8=same(3), b16=same(4), b32=same(5))


def _rwkv_scan_kernel(r_ref, k_ref, v_ref, kk_ref, b_ref, ld_ref, z_ref, rk_ref, gg_ref, gb_ref,
                      o_ref, s_ref):
    @pl.when(pl.program_id(1) == 0)
    def _():
        s_ref[...] = jnp.zeros_like(s_ref)

    d = r_ref.shape[1]
    bdmask = _block_ones()
    ones_bd = jnp.where(bdmask, 1.0, 0.0).astype(BF16)
    mk = _scan_masks()

    def bd(x):
        return jnp.where(bdmask, jnp.tile(x.astype(BF16), (GROUP // HEAD, 1)), 0)

    def pmm(a, b):
        return _dot(a.astype(BF16), bd(b))

    def tri_inv(nm):
        nd = jnp.where(mk["b8"], nm, 0.0)
        n2 = pmm(nd, nd)
        n4 = pmm(n2, n2)
        q = jnp.where(mk["eye"], 1.0, 0.0) + nd
        q = q + pmm(q, n2)
        q = q + pmm(q, n4)
        for lo, hi in (("b8", "b16"), ("b16", "b32"), ("b32", None)):
            off = ~mk[lo] if hi is None else (mk[hi] & ~mk[lo])
            noff = jnp.where(off, nm, 0.0)
            q = q + pmm(pmm(q, noff), q)
        return q

    ti = lax.broadcasted_iota(jnp.int32, (CHUNK, CHUNK), 0)
    tj = lax.broadcasted_iota(jnp.int32, (CHUNK, CHUNK), 1)
    tri = jnp.where(tj <= ti, 1.0, 0.0).astype(BF16)
    ld = ld_ref[...]
    c = sum(_dot(tri, part) for part in _split_bf16(ld, 3))
    c_last = c[CHUNK - 1:CHUNK, :]
    r = r_ref[...]
    k = k_ref[...]
    v = v_ref[...]
    b = b_ref[...]
    rt = r * jnp.exp(c)
    at = -kk_ref[...] * jnp.exp(c - ld)
    p_inv = jnp.exp(-c)
    kt = k * p_inv
    bt = b * p_inv
    p_rest = jnp.exp(c_last - c)
    kh = k * p_rest
    bh = b * p_rest
    p_all = jnp.exp(c_last)

    for g in range(d // GROUP):
        sl = slice(g * GROUP, (g + 1) * GROUP)
        s0 = s_ref[g]
        x = jnp.concatenate([at[:, sl], rt[:, sl]], axis=0).astype(BF16)
        ak = _dot_nt(x, bd(kt[:, sl]))
        ab = _dot_nt(x, bd(bt[:, sl]))
        gs = _dot_nt(x, s0.astype(BF16))
        a_ak = jnp.where(mk["strict"], ak[:CHUNK], 0.0)
        a_rk = jnp.where(mk["incl"], ak[CHUNK:], 0.0)
        nm = jnp.where(mk["strict"], ab[:CHUNK], 0.0)
        a_rb = jnp.where(mk["incl"], ab[CHUNK:], 0.0)
        vg = v[:, sl]
        av = pmm(jnp.concatenate([a_ak, a_rk], axis=0), vg)
        w = gs[:CHUNK] + av[:CHUNK]
        u = pmm(tri_inv(nm), w)
        y = gs[CHUNK:] + av[CHUNK:] + pmm(a_rb, u)
        ds = _dot_tn(jnp.concatenate([vg, u], axis=0).astype(BF16),
                     jnp.concatenate([kh[:, sl], bh[:, sl]], axis=0).astype(BF16))
        s_ref[g] = jnp.where(bdmask, s0 * p_all[:, sl] + ds, 0.0)

        mean = _seg_sum(y, ones_bd) * (1.0 / HEAD)
        yc = y - mean
        var = _seg_sum(yc * yc, ones_bd) * (1.0 / HEAD)
        yn = yc * lax.rsqrt(var + GN_EPS) * gg_ref[:, sl] + gb_ref[:, sl]
        bonus = _seg_sum(r[:, sl] * k[:, sl] * rk_ref[:, sl], ones_bd) * vg
        z = z_ref[:, sl]
        o_ref[:, sl] = ((yn + bonus) * (z * jax.nn.sigmoid(z))).astype(BF16)


def _row_spec(tm, width):
    return pl.BlockSpec((tm, width), lambda i: (i, 0))


def _full_spec(shape):
    return pl.BlockSpec(shape, lambda *_: (0,) * len(shape))


def _rope_tables(seq):
    half = HEAD // 2
    inv = ROPE_THETA ** (-jnp.arange(half, dtype=F32) / half)
    ang = jnp.arange(seq).astype(F32)[:, None] * inv[None, :]
    cos = jnp.cos(ang)
    sin = jnp.sin(ang)
    cos = jnp.tile(jnp.concatenate([cos, cos], axis=-1), (1, 128 // HEAD))
    sin = jnp.tile(jnp.concatenate([-sin, sin], axis=-1), (1, 128 // HEAD))
    return cos, sin


def _attn_in(x2, g, w, bias, seq, tm):
    n, d = x2.shape
    kv = N_KV * HEAD
    cos, sin = _rope_tables(seq)
    per_seq = seq // tm
    tab_spec = pl.BlockSpec((tm, 128), lambda i: (i % per_seq, 0))
    return pl.pallas_call(
        _attn_in_kernel,
        grid=(n // tm,),
        in_specs=[_row_spec(tm, d), _full_spec((1, d)), _full_spec(w.shape),
                  _full_spec((1, w.shape[1])), tab_spec, tab_spec],
        out_specs=[_row_spec(tm, d), _row_spec(tm, kv), _row_spec(tm, kv), _row_spec(tm, d)],
        out_shape=[jax.ShapeDtypeStruct((n, d), BF16), jax.ShapeDtypeStruct((n, kv), BF16),
                   jax.ShapeDtypeStruct((n, kv), BF16), jax.ShapeDtypeStruct((n, d), F32)],
        compiler_params=_cparams(1),
        name="attn_in",
    )(x2, g, w, bias, cos, sin)


def _swa(q, k, v, z, sinks, batch, seq):
    n, d = q.shape
    kv = k.shape[1]
    nb = seq // WINDOW
    cur = lambda width: pl.BlockSpec((WINDOW, width), lambda b, j: (b * nb + j, 0))
    prev = lambda width: pl.BlockSpec(
        (WINDOW, width), lambda b, j: (b * nb + jnp.maximum(j - 1, 0), 0))
    return pl.pallas_call(
        _swa_kernel,
        grid=(batch, nb),
        in_specs=[pl.BlockSpec(memory_space=pltpu.SMEM), cur(d), cur(kv), prev(kv), cur(kv),
                  prev(kv), cur(d)],
        out_specs=cur(d),
        out_shape=jax.ShapeDtypeStruct((n, d), BF16),
        compiler_params=_cparams(2),
        name="swa",
    )(sinks, q, k, k, v, v, z)


def _out_ple(og, h, p2, wo, wg, wp, g, tm, emit_h):
    n, d = h.shape
    out_shape = [jax.ShapeDtypeStruct((n, d), F32)] * (2 if emit_h else 1)
    out_specs = [_row_spec(tm, d)] * (2 if emit_h else 1)
    return pl.pallas_call(
        functools.partial(_out_ple_kernel, emit_h=emit_h),
        grid=(n // tm,),
        in_specs=[_row_spec(tm, d), _row_spec(tm, d), _row_spec(tm, p2.shape[1]),
                  _full_spec(wo.shape), _full_spec(wg.shape), _full_spec(wp.shape),
                  _full_spec((1, d))],
        out_specs=out_specs,
        out_shape=out_shape,
        compiler_params=_cparams(1),
        name="out_ple_mid" if emit_h else "out_ple_last",
    )(og, h, p2, wo, wg, wp, g)


def _rwkv_in(hn, mu, w, w0, w1, w2, a0, a1, a2, k_k, k_a, seq, tm):
    n, d = hn.shape
    prev_spec = pl.BlockSpec((8, d), lambda i: (jnp.maximum(i * (tm // 8) - 1, 0), 0))
    vec = _full_spec((1, d))
    outs = 7
    return pl.pallas_call(
        functools.partial(_rwkv_in_kernel, tiles_per_seq=seq // tm),
        grid=(n // tm,),
        in_specs=[_row_spec(tm, d), prev_spec, _full_spec(mu.shape), _full_spec(w.shape), vec,
                  _full_spec(w1.shape), _full_spec(w2.shape), vec, _full_spec(a1.shape),
                  _full_spec(a2.shape), vec, vec],
        out_specs=[_row_spec(tm, d)] * outs,
        out_shape=[jax.ShapeDtypeStruct((n, d), F32)] * outs,
        compiler_params=_cparams(1),
        name="rwkv_in",
    )(hn, hn, mu, w, w0, w1, w2, a0, a1, a2, k_k, k_a)


def _rwkv_scan(r, k, v, kk, b, ld, z, r_k, gn_g, gn_b, batch, seq):
    n, d = r.shape
    nc = seq // CHUNK
    blk = pl.BlockSpec((CHUNK, d), lambda bi, ci: (bi * nc + ci, 0))
    vec = pl.BlockSpec((1, d), lambda bi, ci: (0, 0))
    return pl.pallas_call(
        _rwkv_scan_kernel,
        grid=(batch, nc),
        in_specs=[blk] * 7 + [vec] * 3,
        out_specs=blk,
        out_shape=jax.ShapeDtypeStruct((n, d), BF16),
        scratch_shapes=[pltpu.VMEM((d // GROUP, GROUP, GROUP), F32)],
        compiler_params=_cparams(2),
        name="rwkv_scan",
    )(r, k, v, kk, b, ld, z, r_k, gn_g, gn_b)


def kernel(x, p, norm_g, attn_w_in, attn_b_in, attn_sinks, attn_w_out, rwkv_mu, rwkv_w_in,
           rwkv_w0, rwkv_w1, rwkv_w2, rwkv_a0, rwkv_a1, rwkv_a2, rwkv_k_k, rwkv_k_a, rwkv_r_k,
           rwkv_gn_g, rwkv_gn_b, rwkv_w_out, ple_w_proj, ple_w_gate, final_norm_g):
    batch, seq, d = x.shape
    depth = p.shape[0]
    assert depth == 2 and d % GROUP == 0 and seq % WINDOW == 0
    n = batch * seq
    tm = min(512, seq)
    tm_r = min(256, seq)
    row = lambda a: a.reshape(1, -1).astype(F32)
    bf = lambda a: a.astype(BF16)

    h = x.reshape(n, d)
    p2 = p.reshape(depth, n, p.shape[-1])

    q, k, v, z = _attn_in(h, row(norm_g[0]), bf(attn_w_in[0]), row(attn_b_in[0]), seq, tm)
    og = _swa(q, k, v, z, attn_sinks[0].astype(F32), batch, seq)
    h, hn = _out_ple(og, h, p2[0], bf(attn_w_out[0]), bf(ple_w_gate[0]), bf(ple_w_proj[0]),
                     row(norm_g[1]), tm, emit_h=True)

    r, k, v, z, kk, b, ld = _rwkv_in(
        hn, rwkv_mu[0].astype(F32), bf(rwkv_w_in[0]), row(rwkv_w0[0]), bf(rwkv_w1[0]),
        bf(rwkv_w2[0]), row(rwkv_a0[0]), bf(rwkv_a1[0]), bf(rwkv_a2[0]), row(rwkv_k_k[0]),
        row(rwkv_k_a[0]), seq, tm_r)
    yg = _rwkv_scan(r, k, v, kk, b, ld, z, row(rwkv_r_k[0]), row(rwkv_gn_g[0]),
                    row(rwkv_gn_b[0]), batch, seq)
    (out,) = _out_ple(yg, h, p2[1], bf(rwkv_w_out[0]), bf(ple_w_gate[1]), bf(ple_w_proj[1]),
                      row(final_norm_g), tm, emit_h=False)
    return out.reshape(batch, seq, d)
```

```python
import functools

import jax
import jax.numpy as jnp
from jax import lax
from jax.experimental import pallas as pl
from jax.experimental.pallas import tpu as pltpu

F32 = jnp.float32
BF16 = jnp.bfloat16

HEAD = 64
N_KV = 4
GQA = 4
WINDOW = 128
ROPE_THETA = 10000.0
MASK_VALUE = -1e30
NORM_EPS = 1e-6
GN_EPS = 64e-5
CHUNK = 64
GROUP = 256
VMEM_LIMIT = 56 * 1024 * 1024


def _cparams(n_axes):
    return pltpu.CompilerParams(
        dimension_semantics=("arbitrary",) * n_axes, vmem_limit_bytes=VMEM_LIMIT)


def _rms(x, g):
    ms = jnp.mean(x * x, axis=-1, keepdims=True)
    return x * lax.rsqrt(ms + NORM_EPS) * g


def _split_bf16(x, terms):
    parts = []
    rem = x
    for _ in range(terms):
        p = rem.astype(BF16)
        parts.append(p)
        rem = rem - p.astype(F32)
    return parts


def _dot(a, b):
    return jnp.dot(a, b, preferred_element_type=F32)


def _dot_nt(a, b):
    return lax.dot_general(a, b, (((1,), (1,)), ((), ())), preferred_element_type=F32)


def _dot_tn(a, b):
    return lax.dot_general(a, b, (((0,), (0,)), ((), ())), preferred_element_type=F32)


def _seg_sum(x, ones_bd):
    hi, lo = _split_bf16(x, 2)
    return _dot(hi, ones_bd) + _dot(lo, ones_bd)


def _block_ones():
    r = lax.broadcasted_iota(jnp.int32, (GROUP, GROUP), 0) >> 6
    c = lax.broadcasted_iota(jnp.int32, (GROUP, GROUP), 1) >> 6
    return r == c


def _attn_in_kernel(x_ref, g_ref, w_ref, b_ref, cos_ref, sin_ref, q_ref, k_ref, v_ref, z_ref):
    hn = _rms(x_ref[...], g_ref[...]).astype(BF16)
    cos = cos_ref[...]
    sin = sin_ref[...]
    first = (lax.broadcasted_iota(jnp.int32, cos.shape, 1) & (HEAD - 1)) < HEAD // 2

    def rope(t):
        rot = jnp.where(first, pltpu.roll(t, 128 - HEAD // 2, 1), pltpu.roll(t, HEAD // 2, 1))
        return t * cos + rot * sin

    d = q_ref.shape[1]
    kv = k_ref.shape[1]
    q = _dot(hn, w_ref[:, 0:d]) + b_ref[:, 0:d]
    for c in range(d // 128):
        sl = slice(c * 128, (c + 1) * 128)
        q_ref[:, sl] = (rope(q[:, sl]) * (HEAD ** -0.5)).astype(BF16)
    k = _dot(hn, w_ref[:, d:d + kv]) + b_ref[:, d:d + kv]
    for c in range(kv // 128):
        sl = slice(c * 128, (c + 1) * 128)
        k_ref[:, sl] = rope(k[:, sl]).astype(BF16)
    v = _dot(hn, w_ref[:, d + kv:d + 2 * kv]) + b_ref[:, d + kv:d + 2 * kv]
    v_ref[...] = v.astype(BF16)
    z_ref[...] = _dot(hn, w_ref[:, d + 2 * kv:]) + b_ref[:, d + 2 * kv:]


def _swa_kernel(sink_ref, q_ref, kc_ref, kp_ref, vc_ref, vp_ref, z_ref, o_ref):
    n = pl.program_id(1)
    qi = lax.broadcasted_iota(jnp.int32, (WINDOW, 2 * WINDOW), 0)
    si = lax.broadcasted_iota(jnp.int32, (WINDOW, 2 * WINDOW), 1)
    diff = qi + WINDOW - si
    mask = (diff >= 0) & (diff < WINDOW) & ((si >= WINDOW) | (n > 0))
    for kh in range(N_KV):
        ks = slice(kh * HEAD, (kh + 1) * HEAD)
        kcat = jnp.concatenate([kp_ref[:, ks], kc_ref[:, ks]], axis=0)
        vcat = jnp.concatenate([vp_ref[:, ks], vc_ref[:, ks]], axis=0)
        outs = []
        for g in range(GQA):
            h = kh * GQA + g
            hs = slice(h * HEAD, (h + 1) * HEAD)
            s = _dot_nt(q_ref[:, hs], kcat)
            s = jnp.where(mask, s, MASK_VALUE)
            sink = sink_ref[h]
            m = jnp.maximum(jnp.max(s, axis=-1, keepdims=True), sink)
            p = jnp.exp(s - m)
            l = jnp.sum(p, axis=-1, keepdims=True) + jnp.exp(sink - m)
            o = _dot(p.astype(BF16), vcat) / l
            z = z_ref[:, hs]
            outs.append(o * (z * jax.nn.sigmoid(z)))
        for j in range(GQA // 2):
            h = kh * GQA + 2 * j
            o_ref[:, h * HEAD:(h + 2) * HEAD] = jnp.concatenate(
                [outs[2 * j], outs[2 * j + 1]], axis=1).astype(BF16)


def _out_ple_kernel(og_ref, h_ref, p_ref, wo_ref, wg_ref, wp_ref, g_ref, *out_refs, emit_h):
    h1 = h_ref[...] + _dot(og_ref[...], wo_ref[...])
    gate = jax.nn.sigmoid(_dot(h1.astype(BF16), wg_ref[...]))
    h2 = h1 + gate * _dot(p_ref[...].astype(BF16), wp_ref[...])
    if emit_h:
        out_refs[0][...] = h2
    out_refs[-1][...] = _rms(h2, g_ref[...])


def _rwkv_in_kernel(hn_ref, prev_ref, mu_ref, w_ref, w0_ref, w1_ref, w2_ref, a0_ref, a1_ref,
                    a2_ref, kk_ref, ka_ref, r_out, k_out, v_out, z_out, kk_out, b_out, ld_out,
                    *, tiles_per_seq):
    i = pl.program_id(0)
    hn = hn_ref[...]
    tm, d = hn.shape
    prev_row = jnp.where(i % tiles_per_seq == 0, 0.0, prev_ref[7:8, :])
    row = lax.broadcasted_iota(jnp.int32, (tm, d), 0)
    shifted = jnp.where(row == 0, prev_row, pltpu.roll(hn, 1, 0))
    xx = shifted - hn

    def lerp(c):
        return (hn + xx * mu_ref[c:c + 1, :]).astype(BF16)

    r_out[...] = _dot(lerp(0), w_ref[:, 0:d])
    k = _dot(lerp(1), w_ref[:, d:2 * d])
    v_out[...] = _dot(lerp(2), w_ref[:, 2 * d:3 * d])
    z_out[...] = _dot(lerp(3), w_ref[:, 3 * d:4 * d])

    lw = jnp.tanh(_dot(lerp(4), w1_ref[...])).astype(BF16)
    u = -(w0_ref[...] + _dot(lw, w2_ref[...]))
    softplus = jnp.maximum(u, 0.0) + jnp.log(1.0 + jnp.exp(-jnp.abs(u)))
    ld_out[...] = -jnp.exp(-softplus - 0.5)

    la = _dot(lerp(5), a1_ref[...]).astype(BF16)
    a = jax.nn.sigmoid(a0_ref[...] + _dot(la, a2_ref[...]))

    ones_bd = jnp.where(_block_ones(), 1.0, 0.0).astype(BF16)
    kk = k * kk_ref[...]
    for g in range(d // GROUP):
        sl = slice(g * GROUP, (g + 1) * GROUP)
        kg = kk[:, sl]
        nrm = jnp.sqrt(_seg_sum(kg * kg, ones_bd))
        kg = kg / jnp.maximum(nrm, 1e-12)
        kk_out[:, sl] = kg
        b_out[:, sl] = kg * a[:, sl]
    k_out[...] = k * (1.0 + (a - 1.0) * ka_ref[...])


def _scan_masks():
    t = lax.broadcasted_iota(jnp.int32, (CHUNK, GROUP), 0)
    s = lax.broadcasted_iota(jnp.int32, (CHUNK, GROUP), 1) & (HEAD - 1)
    same = lambda sh: (t >> sh) == (s >> sh)
    return dict(strict=s < t, incl=s <= t, eye=s == t, b8=same(3), b16=same(4), b32=same(5))


def _rwkv_scan_kernel(r_ref, k_ref, v_ref, kk_ref, b_ref, ld_ref, z_ref, rk_ref, gg_ref, gb_ref,
                      o_ref, s_ref):
    @pl.when(pl.program_id(1) == 0)
    def _():
        s_ref[...] = jnp.zeros_like(s_ref)

    rows, _, d = r_ref.shape
    units = [(i, g) for i in range(rows) for g in range(d // GROUP)]
    lanes = lambda g: slice(g * GROUP, (g + 1) * GROUP)
    each = lambda f: [f(i, g) for i, g in units]
    bdmask = _block_ones()
    ones_bd = jnp.where(bdmask, 1.0, 0.0).astype(BF16)
    mk = _scan_masks()

    def bd(x):
        return jnp.where(bdmask, jnp.tile(x.astype(BF16), (GROUP // HEAD, 1)), 0)

    def pmm(a_list, b_list):
        return [_dot(a.astype(BF16), bd(b)) for a, b in zip(a_list, b_list)]

    def add(a_list, b_list):
        return [a + b for a, b in zip(a_list, b_list)]

    def masked(name, x_list):
        return [jnp.where(mk[name], x, 0.0) for x in x_list]

    def tri_inv(nm):
        nd = masked("b8", nm)
        n2 = pmm(nd, nd)
        n4 = pmm(n2, n2)
        eye = jnp.where(mk["eye"], 1.0, 0.0)
        q = [eye + x for x in nd]
        q = add(q, pmm(q, n2))
        q = add(q, pmm(q, n4))
        for lo, hi in (("b8", "b16"), ("b16", "b32"), ("b32", None)):
            off = ~mk[lo] if hi is None else (mk[hi] & ~mk[lo])
            noff = [jnp.where(off, x, 0.0) for x in nm]
            q = add(q, pmm(pmm(q, noff), q))
        return q

    ti = lax.broadcasted_iota(jnp.int32, (CHUNK, 3 * CHUNK), 0)
    tj = lax.broadcasted_iota(jnp.int32, (CHUNK, 3 * CHUNK), 1) & (CHUNK - 1)
    tri3 = jnp.where(tj <= ti, 1.0, 0.0).astype(BF16)
    pre = []
    for i in range(rows):
        ld = ld_ref[i]
        c = _dot(tri3, jnp.concatenate(_split_bf16(ld, 3), axis=0))
        c_last = c[CHUNK - 1:CHUNK, :]
        r = r_ref[i]
        k = k_ref[i]
        b = b_ref[i]
        p_inv = jnp.exp(-c)
        p_rest = jnp.exp(c_last - c)
        pre.append(dict(
            r=r, k=k, v=v_ref[i], rt=r * jnp.exp(c), at=-kk_ref[i] * jnp.exp(c - ld),
            kt=k * p_inv, bt=b * p_inv, kh=k * p_rest, bh=b * p_rest, p_all=jnp.exp(c_last)))
    get = lambda name: each(lambda i, g: pre[i][name][:, lanes(g)])

    s0 = each(lambda i, g: s_ref[i, g])
    x = each(lambda i, g: jnp.concatenate(
        [pre[i]["at"][:, lanes(g)], pre[i]["rt"][:, lanes(g)]], axis=0).astype(BF16))
    ab = [_dot_nt(xi, bd(t)) for xi, t in zip(x, get("bt"))]
    ak = [_dot_nt(xi, bd(t)) for xi, t in zip(x, get("kt"))]
    gs = [_dot_nt(xi, s.astype(BF16)) for xi, s in zip(x, s0)]
    t_inv = tri_inv(masked("strict", [m[:CHUNK] for m in ab]))
    a_rb = masked("incl", [m[CHUNK:] for m in ab])
    a_k = [jnp.concatenate([jnp.where(mk["strict"], m[:CHUNK], 0.0),
                            jnp.where(mk["incl"], m[CHUNK:], 0.0)], axis=0) for m in ak]
    vg = get("v")
    av = pmm(a_k, vg)
    w = [g_[:CHUNK] + a_[:CHUNK] for g_, a_ in zip(gs, av)]
    u = pmm(t_inv, w)
    y = [g_[CHUNK:] + a_[CHUNK:] + c_ for g_, a_, c_ in zip(gs, av, pmm(a_rb, u))]
    ds = [_dot_tn(jnp.concatenate([v_, u_], axis=0).astype(BF16),
                  jnp.concatenate([kh_, bh_], axis=0).astype(BF16))
          for v_, u_, kh_, bh_ in zip(vg, u, get("kh"), get("bh"))]
    for (i, g), s_, ds_ in zip(units, s0, ds):
        s_ref[i, g] = jnp.where(bdmask, s_ * pre[i]["p_all"][:, lanes(g)] + ds_, 0.0)

    rkr = each(lambda i, g: pre[i]["r"][:, lanes(g)] * pre[i]["k"][:, lanes(g)] * rk_ref[:, lanes(g)])
    st1 = [_dot(jnp.concatenate(_split_bf16(y_, 2) + _split_bf16(q_, 2), axis=0), ones_bd)
           for y_, q_ in zip(y, rkr)]
    mean = [(s_[:CHUNK] + s_[CHUNK:2 * CHUNK]) * (1.0 / HEAD) for s_ in st1]
    yc = [y_ - m_ for y_, m_ in zip(y, mean)]
    st2 = [_dot(jnp.concatenate(_split_bf16(c_ * c_, 2), axis=0), ones_bd) for c_ in yc]
    for (i, g), yc_, s1, s2, v_ in zip(units, yc, st1, st2, vg):
        var = (s2[:CHUNK] + s2[CHUNK:]) * (1.0 / HEAD)
        yn = yc_ * lax.rsqrt(var + GN_EPS) * gg_ref[:, lanes(g)] + gb_ref[:, lanes(g)]
        bonus = (s1[2 * CHUNK:3 * CHUNK] + s1[3 * CHUNK:]) * v_
        z = z_ref[i][:, lanes(g)]
        o_ref[i, :, lanes(g)] = ((yn + bonus) * (z * jax.nn.sigmoid(z))).astype(BF16)


def _row_spec(tm, width):
    return pl.BlockSpec((tm, width), lambda i: (i, 0))


def _full_spec(shape):
    return pl.BlockSpec(shape, lambda *_: (0,) * len(shape))


def _rope_tables(seq):
    half = HEAD // 2
    inv = ROPE_THETA ** (-jnp.arange(half, dtype=F32) / half)
    ang = jnp.arange(seq).astype(F32)[:, None] * inv[None, :]
    cos = jnp.cos(ang)
    sin = jnp.sin(ang)
    cos = jnp.tile(jnp.concatenate([cos, cos], axis=-1), (1, 128 // HEAD))
    sin = jnp.tile(jnp.concatenate([-sin, sin], axis=-1), (1, 128 // HEAD))
    return cos, sin


def _attn_in(x2, g, w, bias, seq, tm):
    n, d = x2.shape
    kv = N_KV * HEAD
    cos, sin = _rope_tables(seq)
    per_seq = seq // tm
    tab_spec = pl.BlockSpec((tm, 128), lambda i: (i % per_seq, 0))
    return pl.pallas_call(
        _attn_in_kernel,
        grid=(n // tm,),
        in_specs=[_row_spec(tm, d), _full_spec((1, d)), _full_spec(w.shape),
                  _full_spec((1, w.shape[1])), tab_spec, tab_spec],
        out_specs=[_row_spec(tm, d), _row_spec(tm, kv), _row_spec(tm, kv), _row_spec(tm, d)],
        out_shape=[jax.ShapeDtypeStruct((n, d), BF16), jax.ShapeDtypeStruct((n, kv), BF16),
                   jax.ShapeDtypeStruct((n, kv), BF16), jax.ShapeDtypeStruct((n, d), F32)],
        compiler_params=_cparams(1),
        name="attn_in",
    )(x2, g, w, bias, cos, sin)


def _swa(q, k, v, z, sinks, batch, seq):
    n, d = q.shape
    kv = k.shape[1]
    nb = seq // WINDOW
    cur = lambda width: pl.BlockSpec((WINDOW, width), lambda b, j: (b * nb + j, 0))
    prev = lambda width: pl.BlockSpec(
        (WINDOW, width), lambda b, j: (b * nb + jnp.maximum(j - 1, 0), 0))
    return pl.pallas_call(
        _swa_kernel,
        grid=(batch, nb),
        in_specs=[pl.BlockSpec(memory_space=pltpu.SMEM), cur(d), cur(kv), prev(kv), cur(kv),
                  prev(kv), cur(d)],
        out_specs=cur(d),
        out_shape=jax.ShapeDtypeStruct((n, d), BF16),
        compiler_params=_cparams(2),
        name="swa",
    )(sinks, q, k, k, v, v, z)


def _out_ple(og, h, p2, wo, wg, wp, g, tm, emit_h):
    n, d = h.shape
    out_shape = [jax.ShapeDtypeStruct((n, d), F32)] * (2 if emit_h else 1)
    out_specs = [_row_spec(tm, d)] * (2 if emit_h else 1)
    return pl.pallas_call(
        functools.partial(_out_ple_kernel, emit_h=emit_h),
        grid=(n // tm,),
        in_specs=[_row_spec(tm, d), _row_spec(tm, d), _row_spec(tm, p2.shape[1]),
                  _full_spec(wo.shape), _full_spec(wg.shape), _full_spec(wp.shape),
                  _full_spec((1, d))],
        out_specs=out_specs,
        out_shape=out_shape,
        compiler_params=_cparams(1),
        name="out_ple_mid" if emit_h else "out_ple_last",
    )(og, h, p2, wo, wg, wp, g)


def _rwkv_in(hn, mu, w, w0, w1, w2, a0, a1, a2, k_k, k_a, seq, tm):
    n, d = hn.shape
    prev_spec = pl.BlockSpec((8, d), lambda i: (jnp.maximum(i * (tm // 8) - 1, 0), 0))
    vec = _full_spec((1, d))
    outs = 7
    return pl.pallas_call(
        functools.partial(_rwkv_in_kernel, tiles_per_seq=seq // tm),
        grid=(n // tm,),
        in_specs=[_row_spec(tm, d), prev_spec, _full_spec(mu.shape), _full_spec(w.shape), vec,
                  _full_spec(w1.shape), _full_spec(w2.shape), vec, _full_spec(a1.shape),
                  _full_spec(a2.shape), vec, vec],
        out_specs=[_row_spec(tm, d)] * outs,
        out_shape=[jax.ShapeDtypeStruct((n, d), F32)] * outs,
        compiler_params=_cparams(1),
        name="rwkv_in",
    )(hn, hn, mu, w, w0, w1, w2, a0, a1, a2, k_k, k_a)


def _rwkv_scan(r, k, v, kk, b, ld, z, r_k, gn_g, gn_b, batch, seq, rows):
    n, d = r.shape
    nc = seq // CHUNK
    blk = pl.BlockSpec((rows, CHUNK, d), lambda bi, ci: (bi, ci, 0))
    vec = pl.BlockSpec((1, d), lambda bi, ci: (0, 0))
    seqs = [a.reshape(batch, seq, d) for a in (r, k, v, kk, b, ld, z)]
    out = pl.pallas_call(
        _rwkv_scan_kernel,
        grid=(batch // rows, nc),
        in_specs=[blk] * 7 + [vec] * 3,
        out_specs=blk,
        out_shape=jax.ShapeDtypeStruct((batch, seq, d), BF16),
        scratch_shapes=[pltpu.VMEM((rows, d // GROUP, GROUP, GROUP), F32)],
        compiler_params=_cparams(2),
        name="rwkv_scan",
    )(*seqs, r_k, gn_g, gn_b)
    return out.reshape(n, d)


def kernel(x, p, norm_g, attn_w_in, attn_b_in, attn_sinks, attn_w_out, rwkv_mu, rwkv_w_in,
           rwkv_w0, rwkv_w1, rwkv_w2, rwkv_a0, rwkv_a1, rwkv_a2, rwkv_k_k, rwkv_k_a, rwkv_r_k,
           rwkv_gn_g, rwkv_gn_b, rwkv_w_out, ple_w_proj, ple_w_gate, final_norm_g):
    batch, seq, d = x.shape
    depth = p.shape[0]
    assert depth == 2 and d % GROUP == 0 and seq % WINDOW == 0
    n = batch * seq
    tm = min(512, seq)
    tm_r = min(256, seq)
    row = lambda a: a.reshape(1, -1).astype(F32)
    bf = lambda a: a.astype(BF16)

    h = x.reshape(n, d)
    p2 = p.reshape(depth, n, p.shape[-1])

    q, k, v, z = _attn_in(h, row(norm_g[0]), bf(attn_w_in[0]), row(attn_b_in[0]), seq, tm)
    og = _swa(q, k, v, z, attn_sinks[0].astype(F32), batch, seq)
    h, hn = _out_ple(og, h, p2[0], bf(attn_w_out[0]), bf(ple_w_gate[0]), bf(ple_w_proj[0]),
                     row(norm_g[1]), tm, emit_h=True)

    r, k, v, z, kk, b, ld = _rwkv_in(
        hn, rwkv_mu[0].astype(F32), bf(rwkv_w_in[0]), row(rwkv_w0[0]), bf(rwkv_w1[0]),
        bf(rwkv_w2[0]), row(rwkv_a0[0]), bf(rwkv_a1[0]), bf(rwkv_a2[0]), row(rwkv_k_k[0]),
        row(rwkv_k_a[0]), seq, tm_r)
    yg = _rwkv_scan(r, k, v, kk, b, ld, z, row(rwkv_r_k[0]), row(rwkv_gn_g[0]),
                    row(rwkv_gn_b[0]), batch, seq, rows=2 if batch % 2 == 0 else 1)
    (out,) = _out_ple(yg, h, p2[1], bf(rwkv_w_out[0]), bf(ple_w_gate[1]), bf(ple_w_proj[1]),
                      row(final_norm_g), tm, emit_h=False)
    return out.reshape(batch, seq, d)
```

```python
import functools

import jax
import jax.numpy as jnp
from jax import lax
from jax.experimental import pallas as pl
from jax.experimental.pallas import tpu as pltpu

F32 = jnp.float32
BF16 = jnp.bfloat16

HEAD = 64
N_KV = 4
GQA = 4
WINDOW = 128
ROPE_THETA = 10000.0
MASK_VALUE = -1e30
NORM_EPS = 1e-6
GN_EPS = 64e-5
CHUNK = 64
GROUP = 256
SCAN_ROWS = 4
SWA_BLOCKS = 4
VMEM_LIMIT = 56 * 1024 * 1024


def _cparams(n_axes):
    return pltpu.CompilerParams(
        dimension_semantics=("arbitrary",) * n_axes, vmem_limit_bytes=VMEM_LIMIT)


def _rms(x, g):
    ms = jnp.mean(x * x, axis=-1, keepdims=True)
    return x * lax.rsqrt(ms + NORM_EPS) * g


def _split_bf16(x, terms):
    parts = []
    rem = x
    for _ in range(terms):
        p = rem.astype(BF16)
        parts.append(p)
        rem = rem - p.astype(F32)
    return parts


def _dot(a, b):
    return jnp.dot(a, b, preferred_element_type=F32)


def _dot_nt(a, b):
    return lax.dot_general(a, b, (((1,), (1,)), ((), ())), preferred_element_type=F32)


def _dot_tn(a, b):
    return lax.dot_general(a, b, (((0,), (0,)), ((), ())), preferred_element_type=F32)


def _seg_sum(x, ones_bd):
    hi, lo = _split_bf16(x, 2)
    return _dot(hi, ones_bd) + _dot(lo, ones_bd)


def _block_ones():
    r = lax.broadcasted_iota(jnp.int32, (GROUP, GROUP), 0) >> 6
    c = lax.broadcasted_iota(jnp.int32, (GROUP, GROUP), 1) >> 6
    return r == c


def _attn_in_kernel(x_ref, g_ref, w_ref, b_ref, cos_ref, sin_ref, q_ref, k_ref, v_ref, z_ref):
    hn = _rms(x_ref[...], g_ref[...]).astype(BF16)
    cos = cos_ref[...]
    sin = sin_ref[...]
    first = (lax.broadcasted_iota(jnp.int32, cos.shape, 1) & (HEAD - 1)) < HEAD // 2

    def rope(t):
        rot = jnp.where(first, pltpu.roll(t, 128 - HEAD // 2, 1), pltpu.roll(t, HEAD // 2, 1))
        return t * cos + rot * sin

    d = q_ref.shape[1]
    kv = k_ref.shape[1]
    q = _dot(hn, w_ref[:, 0:d]) + b_ref[:, 0:d]
    for c in range(d // 128):
        sl = slice(c * 128, (c + 1) * 128)
        q_ref[:, sl] = (rope(q[:, sl]) * (HEAD ** -0.5)).astype(BF16)
    k = _dot(hn, w_ref[:, d:d + kv]) + b_ref[:, d:d + kv]
    for c in range(kv // 128):
        sl = slice(c * 128, (c + 1) * 128)
        k_ref[:, sl] = rope(k[:, sl]).astype(BF16)
    v = _dot(hn, w_ref[:, d + kv:d + 2 * kv]) + b_ref[:, d + kv:d + 2 * kv]
    v_ref[...] = v.astype(BF16)
    z = _dot(hn, w_ref[:, d + 2 * kv:]) + b_ref[:, d + 2 * kv:]
    z_ref[...] = z * jax.nn.sigmoid(z)


def _swa_kernel(sink_ref, q_ref, kc_ref, kp_ref, vc_ref, vp_ref, zg_ref, o_ref):
    n = pl.program_id(1)
    blocks = q_ref.shape[0] // WINDOW
    si = lax.broadcasted_iota(jnp.int32, (2 * WINDOW, WINDOW), 0)
    qi = lax.broadcasted_iota(jnp.int32, (2 * WINDOW, WINDOW), 1)
    diff = qi + WINDOW - si
    band = (diff >= 0) & (diff < WINDOW)
    first_mask = band & ((si >= WINDOW) | (n > 0))
    kv_slices = [slice(kh * HEAD, (kh + 1) * HEAD) for kh in range(N_KV)]
    units = [(j, h) for j in range(blocks) for h in range(N_KV * GQA)]

    def window(cur_ref, prev_ref, j, ks):
        if j == 0:
            return jnp.concatenate([prev_ref[:, ks], cur_ref[0:WINDOW, ks]], axis=0)
        return cur_ref[(j - 1) * WINDOW:(j + 1) * WINDOW, ks]

    kcat = [[window(kc_ref, kp_ref, j, ks) for ks in kv_slices] for j in range(blocks)]
    vcat_t = [[window(vc_ref, vp_ref, j, ks).T for ks in kv_slices] for j in range(blocks)]
    s = [_dot_nt(kcat[j][h // GQA], q_ref[j * WINDOW:(j + 1) * WINDOW, h * HEAD:(h + 1) * HEAD])
         for j, h in units]
    s = [jnp.where(first_mask if j == 0 else band, x, MASK_VALUE) for (j, h), x in zip(units, s)]
    m = [jnp.maximum(jnp.max(x, axis=0, keepdims=True), sink_ref[h]) for (j, h), x in zip(units, s)]
    p = [jnp.exp(x - m_) for x, m_ in zip(s, m)]
    l = [jnp.sum(x, axis=0, keepdims=True) + jnp.exp(sink_ref[h] - m_)
         for (j, h), x, m_ in zip(units, p, m)]
    o_t = [_dot(vcat_t[j][h // GQA], x.astype(BF16)) / l_
           for (j, h), x, l_ in zip(units, p, l)]
    nh = N_KV * GQA
    for j in range(blocks):
        rows = slice(j * WINDOW, (j + 1) * WINDOW)
        o = jnp.concatenate(o_t[j * nh:(j + 1) * nh], axis=0).T
        o_ref[rows, :] = (o * zg_ref[rows, :]).astype(BF16)


def _out_ple_kernel(og_ref, h_ref, p_ref, wo_ref, wg_ref, wp_ref, g_ref, *out_refs, emit_h):
    h1 = h_ref[...] + _dot(og_ref[...], wo_ref[...])
    gate = jax.nn.sigmoid(_dot(h1.astype(BF16), wg_ref[...]))
    h2 = h1 + gate * _dot(p_ref[...].astype(BF16), wp_ref[...])
    if emit_h:
        out_refs[0][...] = h2
    out_refs[-1][...] = _rms(h2, g_ref[...])


def _rwkv_in_kernel(hn_ref, prev_ref, mu_ref, w_ref, w0_ref, w1_ref, w2_ref, a0_ref, a1_ref,
                    a2_ref, kk_ref, ka_ref, r_out, k_out, v_out, z_out, kk_out, b_out, ld_out,
                    *, tiles_per_seq):
    i = pl.program_id(0)
    hn = hn_ref[...]
    tm, d = hn.shape
    prev_row = jnp.where(i % tiles_per_seq == 0, 0.0, prev_ref[7:8, :])
    row = lax.broadcasted_iota(jnp.int32, (tm, d), 0)
    shifted = jnp.where(row == 0, prev_row, pltpu.roll(hn, 1, 0))
    xx = shifted - hn

    def lerp(c):
        return (hn + xx * mu_ref[c:c + 1, :]).astype(BF16)

    r_out[...] = _dot(lerp(0), w_ref[:, 0:d])
    k = _dot(lerp(1), w_ref[:, d:2 * d])
    v_out[...] = _dot(lerp(2), w_ref[:, 2 * d:3 * d])
    z_out[...] = _dot(lerp(3), w_ref[:, 3 * d:4 * d])

    lw = jnp.tanh(_dot(lerp(4), w1_ref[...])).astype(BF16)
    u = -(w0_ref[...] + _dot(lw, w2_ref[...]))
    softplus = jnp.maximum(u, 0.0) + jnp.log(1.0 + jnp.exp(-jnp.abs(u)))
    ld_out[...] = -jnp.exp(-softplus - 0.5)

    la = _dot(lerp(5), a1_ref[...]).astype(BF16)
    a = jax.nn.sigmoid(a0_ref[...] + _dot(la, a2_ref[...]))

    ones_bd = jnp.where(_block_ones(), 1.0, 0.0).astype(BF16)
    kk = k * kk_ref[...]
    for g in range(d // GROUP):
        sl = slice(g * GROUP, (g + 1) * GROUP)
        kg = kk[:, sl]
        nrm = jnp.sqrt(_seg_sum(kg * kg, ones_bd))
        kg = kg / jnp.maximum(nrm, 1e-12)
        kk_out[:, sl] = kg
        b_out[:, sl] = kg * a[:, sl]
    k_out[...] = k * (1.0 + (a - 1.0) * ka_ref[...])


def _scan_masks():
    t = lax.broadcasted_iota(jnp.int32, (CHUNK, GROUP), 0)
    s = lax.broadcasted_iota(jnp.int32, (CHUNK, GROUP), 1) & (HEAD - 1)
    same = lambda sh: (t >> sh) == (s >> sh)
    return dict(strict=s < t, incl=s <= t, eye=s == t, b8=same(3), b16=same(4), b32=same(5))


def _rwkv_scan_kernel(r_ref, k_ref, v_ref, kk_ref, b_ref, ld_ref, z_ref, rk_ref, gg_ref, gb_ref,
                      o_ref, s_ref):
    @pl.when(pl.program_id(1) == 0)
    def _():
        s_ref[...] = jnp.zeros_like(s_ref)

    rows, _, d = r_ref.shape
    units = [(i, g) for i in range(rows) for g in range(d // GROUP)]
    lanes = lambda g: slice(g * GROUP, (g + 1) * GROUP)
    each = lambda f: [f(i, g) for i, g in units]
    bdmask = _block_ones()
    ones_bd = jnp.where(bdmask, 1.0, 0.0).astype(BF16)
    mk = _scan_masks()

    def bd(x):
        return jnp.where(bdmask, jnp.tile(x.astype(BF16), (GROUP // HEAD, 1)), 0)

    def pmm(a_list, b_list):
        return [_dot(a.astype(BF16), bd(b)) for a, b in zip(a_list, b_list)]

    def add(a_list, b_list):
        return [a + b for a, b in zip(a_list, b_list)]

    def masked(name, x_list):
        return [jnp.where(mk[name], x, 0.0) for x in x_list]

    def tri_inv(nm):
        nd = masked("b8", nm)
        n2 = pmm(nd, nd)
        n4 = pmm(n2, n2)
        eye = jnp.where(mk["eye"], 1.0, 0.0)
        q = [eye + x for x in nd]
        q = add(q, pmm(q, n2))
        q = add(q, pmm(q, n4))
        nm_bd = [bd(x) for x in nm]
        for lo, hi in (("b8", "b16"), ("b16", "b32"), ("b32", None)):
            off = ~mk[lo] if hi is None else (mk[hi] & ~mk[lo])
            tn = [jnp.where(off, _dot(t.astype(BF16), w), 0.0) for t, w in zip(q, nm_bd)]
            q = add(q, pmm(tn, q))
        return q

    ti = lax.broadcasted_iota(jnp.int32, (CHUNK, 3 * CHUNK), 0)
    tj = lax.broadcasted_iota(jnp.int32, (CHUNK, 3 * CHUNK), 1) & (CHUNK - 1)
    tri3 = jnp.where(tj <= ti, 1.0, 0.0).astype(BF16)
    pre = []
    for i in range(rows):
        ld = ld_ref[i]
        c = _dot(tri3, jnp.concatenate(_split_bf16(ld, 3), axis=0))
        c_last = c[CHUNK - 1:CHUNK, :]
        r = r_ref[i]
        k = k_ref[i]
        b = b_ref[i]
        p_inv = jnp.exp(-c)
        p_rest = jnp.exp(c_last - c)
        pre.append(dict(
            r=r, k=k, v=v_ref[i], rt=r * jnp.exp(c), at=-kk_ref[i] * jnp.exp(c - ld),
            kt=k * p_inv, bt=b * p_inv, kh=k * p_rest, bh=b * p_rest, p_all=jnp.exp(c_last)))
    get = lambda name: each(lambda i, g: pre[i][name][:, lanes(g)])

    s0 = each(lambda i, g: s_ref[i, g])
    x = each(lambda i, g: jnp.concatenate(
        [pre[i]["at"][:, lanes(g)], pre[i]["rt"][:, lanes(g)]], axis=0).astype(BF16))
    ab = [_dot_nt(xi, bd(t)) for xi, t in zip(x, get("bt"))]
    ak = [_dot_nt(xi, bd(t)) for xi, t in zip(x, get("kt"))]
    gs = [_dot_nt(xi, s.astype(BF16)) for xi, s in zip(x, s0)]
    t_inv = tri_inv(masked("strict", [m[:CHUNK] for m in ab]))
    a_rb = masked("incl", [m[CHUNK:] for m in ab])
    a_k = [jnp.concatenate([jnp.where(mk["strict"], m[:CHUNK], 0.0),
                            jnp.where(mk["incl"], m[CHUNK:], 0.0)], axis=0) for m in ak]
    vg = get("v")
    av = pmm(a_k, vg)
    w = [g_[:CHUNK] + a_[:CHUNK] for g_, a_ in zip(gs, av)]
    u = pmm(t_inv, w)
    y = [g_[CHUNK:] + a_[CHUNK:] + c_ for g_, a_, c_ in zip(gs, av, pmm(a_rb, u))]
    ds = [_dot_tn(jnp.concatenate([v_, u_], axis=0).astype(BF16),
                  jnp.concatenate([kh_, bh_], axis=0).astype(BF16))
          for v_, u_, kh_, bh_ in zip(vg, u, get("kh"), get("bh"))]
    for (i, g), s_, ds_ in zip(units, s0, ds):
        s_ref[i, g] = jnp.where(bdmask, s_ * pre[i]["p_all"][:, lanes(g)] + ds_, 0.0)

    rkr = each(lambda i, g: pre[i]["r"][:, lanes(g)] * pre[i]["k"][:, lanes(g)] * rk_ref[:, lanes(g)])
    st1 = [_dot(jnp.concatenate(_split_bf16(y_, 2) + _split_bf16(q_, 2), axis=0), ones_bd)
           for y_, q_ in zip(y, rkr)]
    mean = [(s_[:CHUNK] + s_[CHUNK:2 * CHUNK]) * (1.0 / HEAD) for s_ in st1]
    yc = [y_ - m_ for y_, m_ in zip(y, mean)]
    st2 = [_dot(jnp.concatenate(_split_bf16(c_ * c_, 2), axis=0), ones_bd) for c_ in yc]
    for (i, g), yc_, s1, s2, v_ in zip(units, yc, st1, st2, vg):
        var = (s2[:CHUNK] + s2[CHUNK:]) * (1.0 / HEAD)
        yn = yc_ * lax.rsqrt(var + GN_EPS) * gg_ref[:, lanes(g)] + gb_ref[:, lanes(g)]
        bonus = (s1[2 * CHUNK:3 * CHUNK] + s1[3 * CHUNK:]) * v_
        z = z_ref[i][:, lanes(g)]
        o_ref[i, :, lanes(g)] = ((yn + bonus) * (z * jax.nn.sigmoid(z))).astype(BF16)


def _row_spec(tm, width):
    return pl.BlockSpec((tm, width), lambda i: (i, 0))


def _full_spec(shape):
    return pl.BlockSpec(shape, lambda *_: (0,) * len(shape))


def _rope_tables(seq):
    half = HEAD // 2
    inv = ROPE_THETA ** (-jnp.arange(half, dtype=F32) / half)
    ang = jnp.arange(seq).astype(F32)[:, None] * inv[None, :]
    cos = jnp.cos(ang)
    sin = jnp.sin(ang)
    cos = jnp.tile(jnp.concatenate([cos, cos], axis=-1), (1, 128 // HEAD))
    sin = jnp.tile(jnp.concatenate([-sin, sin], axis=-1), (1, 128 // HEAD))
    return cos, sin


def _attn_in(x2, g, w, bias, seq, tm):
    n, d = x2.shape
    kv = N_KV * HEAD
    cos, sin = _rope_tables(seq)
    per_seq = seq // tm
    tab_spec = pl.BlockSpec((tm, 128), lambda i: (i % per_seq, 0))
    return pl.pallas_call(
        _attn_in_kernel,
        grid=(n // tm,),
        in_specs=[_row_spec(tm, d), _full_spec((1, d)), _full_spec(w.shape),
                  _full_spec((1, w.shape[1])), tab_spec, tab_spec],
        out_specs=[_row_spec(tm, d), _row_spec(tm, kv), _row_spec(tm, kv), _row_spec(tm, d)],
        out_shape=[jax.ShapeDtypeStruct((n, d), BF16), jax.ShapeDtypeStruct((n, kv), BF16),
                   jax.ShapeDtypeStruct((n, kv), BF16), jax.ShapeDtypeStruct((n, d), F32)],
        compiler_params=_cparams(1),
        name="attn_in",
    )(x2, g, w, bias, cos, sin)


def _swa(q, k, v, z, sinks, batch, seq):
    n, d = q.shape
    kv = k.shape[1]
    blocks = max(b_ for b_ in range(1, SWA_BLOCKS + 1) if (seq // WINDOW) % b_ == 0)
    nb = seq // (blocks * WINDOW)
    cur = lambda width: pl.BlockSpec((blocks * WINDOW, width), lambda b, j: (b * nb + j, 0))
    prev = lambda width: pl.BlockSpec(
        (WINDOW, width), lambda b, j: ((b * nb + j) * blocks - jnp.minimum(j, 1), 0))
    return pl.pallas_call(
        _swa_kernel,
        grid=(batch, nb),
        in_specs=[pl.BlockSpec(memory_space=pltpu.SMEM), cur(d), cur(kv), prev(kv), cur(kv),
                  prev(kv), cur(d)],
        out_specs=cur(d),
        out_shape=jax.ShapeDtypeStruct((n, d), BF16),
        compiler_params=_cparams(2),
        name="swa",
    )(sinks, q, k, k, v, v, z)


def _out_ple(og, h, p3, layer, wo, wg, wp, g, tm, emit_h):
    n, d = h.shape
    out_shape = [jax.ShapeDtypeStruct((n, d), F32)] * (2 if emit_h else 1)
    out_specs = [_row_spec(tm, d)] * (2 if emit_h else 1)
    return pl.pallas_call(
        functools.partial(_out_ple_kernel, emit_h=emit_h),
        grid=(n // tm,),
        in_specs=[_row_spec(tm, d), _row_spec(tm, d),
                  pl.BlockSpec((None, tm, p3.shape[2]), lambda i: (layer, i, 0)),
                  _full_spec(wo.shape), _full_spec(wg.shape), _full_spec(wp.shape),
                  _full_spec((1, d))],
        out_specs=out_specs,
        out_shape=out_shape,
        compiler_params=_cparams(1),
        name="out_ple_mid" if emit_h else "out_ple_last",
    )(og, h, p3, wo, wg, wp, g)


def _rwkv_in(hn, mu, w, w0, w1, w2, a0, a1, a2, k_k, k_a, seq, tm):
    n, d = hn.shape
    prev_spec = pl.BlockSpec((8, d), lambda i: (jnp.maximum(i * (tm // 8) - 1, 0), 0))
    vec = _full_spec((1, d))
    outs = 7
    return pl.pallas_call(
        functools.partial(_rwkv_in_kernel, tiles_per_seq=seq // tm),
        grid=(n // tm,),
        in_specs=[_row_spec(tm, d), prev_spec, _full_spec(mu.shape), _full_spec(w.shape), vec,
                  _full_spec(w1.shape), _full_spec(w2.shape), vec, _full_spec(a1.shape),
                  _full_spec(a2.shape), vec, vec],
        out_specs=[_row_spec(tm, d)] * outs,
        out_shape=[jax.ShapeDtypeStruct((n, d), F32)] * outs,
        compiler_params=_cparams(1),
        name="rwkv_in",
    )(hn, hn, mu, w, w0, w1, w2, a0, a1, a2, k_k, k_a)


def _rwkv_scan(r, k, v, kk, b, ld, z, r_k, gn_g, gn_b, batch, seq, rows):
    n, d = r.shape
    nc = seq // CHUNK
    blk = pl.BlockSpec((rows, CHUNK, d), lambda bi, ci: (bi, ci, 0))
    vec = pl.BlockSpec((1, d), lambda bi, ci: (0, 0))
    seqs = [a.reshape(batch, seq, d) for a in (r, k, v, kk, b, ld, z)]
    out = pl.pallas_call(
        _rwkv_scan_kernel,
        grid=(batch // rows, nc),
        in_specs=[blk] * 7 + [vec] * 3,
        out_specs=blk,
        out_shape=jax.ShapeDtypeStruct((batch, seq, d), BF16),
        scratch_shapes=[pltpu.VMEM((rows, d // GROUP, GROUP, GROUP), F32)],
        compiler_params=_cparams(2),
        name="rwkv_scan",
    )(*seqs, r_k, gn_g, gn_b)
    return out.reshape(n, d)


def kernel(x, p, norm_g, attn_w_in, attn_b_in, attn_sinks, attn_w_out, rwkv_mu, rwkv_w_in,
           rwkv_w0, rwkv_w1, rwkv_w2, rwkv_a0, rwkv_a1, rwkv_a2, rwkv_k_k, rwkv_k_a, rwkv_r_k,
           rwkv_gn_g, rwkv_gn_b, rwkv_w_out, ple_w_proj, ple_w_gate, final_norm_g):
    batch, seq, d = x.shape
    depth = p.shape[0]
    assert depth == 2 and d % GROUP == 0 and seq % WINDOW == 0
    n = batch * seq
    tm = min(512, seq)
    tm_r = min(256, seq)
    row = lambda a: a.reshape(1, -1).astype(F32)
    bf = lambda a: a.astype(BF16)

    h = x.reshape(n, d)
    p3 = p.reshape(depth, n, p.shape[-1])

    q, k, v, z = _attn_in(h, row(norm_g[0]), bf(attn_w_in[0]), row(attn_b_in[0]), seq, tm)
    og = _swa(q, k, v, z, attn_sinks[0].astype(F32), batch, seq)
    h, hn = _out_ple(og, h, p3, 0,bf(attn_w_out[0]), bf(ple_w_gate[0]), bf(ple_w_proj[0]),
                     row(norm_g[1]), tm, emit_h=True)

    r, k, v, z, kk, b, ld = _rwkv_in(
        hn, rwkv_mu[0].astype(F32), bf(rwkv_w_in[0]), row(rwkv_w0[0]), bf(rwkv_w1[0]),
        bf(rwkv_w2[0]), row(rwkv_a0[0]), bf(rwkv_a1[0]), bf(rwkv_a2[0]), row(rwkv_k_k[0]),
        row(rwkv_k_a[0]), seq, tm_r)
    yg = _rwkv_scan(r, k, v, kk, b, ld, z, row(rwkv_r_k[0]), row(rwkv_gn_g[0]),
                    row(rwkv_gn_b[0]), batch, seq,
                    rows=max(r_ for r_ in range(1, SCAN_ROWS + 1) if batch % r_ == 0))
    (out,) = _out_ple(yg, h, p3, 1,bf(rwkv_w_out[0]), bf(ple_w_gate[1]), bf(ple_w_proj[1]),
                      row(final_norm_g), tm, emit_h=False)
    return out.reshape(batch, seq, d)
```

```python
import functools
import math

import jax
import jax.numpy as jnp
from jax import lax
from jax.experimental import pallas as pl
from jax.experimental.pallas import tpu as pltpu

F32 = jnp.float32
BF16 = jnp.bfloat16

HEAD = 64
N_KV = 4
GQA = 4
WINDOW = 128
ROPE_THETA = 10000.0
MASK_VALUE = -1e30
NORM_EPS = 1e-6
GN_EPS = 64e-5
DECAY_SCALE = math.exp(-0.5)
CHUNK = 64
GROUP = 128
SCAN_ROWS = 4
SWA_BLOCKS = 4
VMEM_LIMIT = 56 * 1024 * 1024


def _cparams(n_axes):
    return pltpu.CompilerParams(
        dimension_semantics=("arbitrary",) * n_axes, vmem_limit_bytes=VMEM_LIMIT)


def _rms(x, g):
    ms = jnp.mean(x * x, axis=-1, keepdims=True)
    return x * lax.rsqrt(ms + NORM_EPS) * g


def _split_bf16(x, terms):
    parts = []
    rem = x
    for _ in range(terms):
        p = rem.astype(BF16)
        parts.append(p)
        rem = rem - p.astype(F32)
    return parts


def _dot(a, b):
    return jnp.dot(a, b, preferred_element_type=F32)


def _dot_nt(a, b):
    return lax.dot_general(a, b, (((1,), (1,)), ((), ())), preferred_element_type=F32)


def _dot_tn(a, b):
    return lax.dot_general(a, b, (((0,), (0,)), ((), ())), preferred_element_type=F32)


def _seg_sum(x, ones_bd):
    hi, lo = _split_bf16(x, 2)
    return _dot(hi, ones_bd) + _dot(lo, ones_bd)


def _block_ones():
    r = lax.broadcasted_iota(jnp.int32, (GROUP, GROUP), 0) >> 6
    c = lax.broadcasted_iota(jnp.int32, (GROUP, GROUP), 1) >> 6
    return r == c


def _attn_in_kernel(x_ref, g_ref, w_ref, b_ref, cos_ref, sin_ref, q_ref, k_ref, v_ref, z_ref):
    hn = _rms(x_ref[...], g_ref[...]).astype(BF16)
    cos = cos_ref[...]
    sin = sin_ref[...]
    first = (lax.broadcasted_iota(jnp.int32, cos.shape, 1) & (HEAD - 1)) < HEAD // 2

    def rope(t):
        rot = jnp.where(first, pltpu.roll(t, 128 - HEAD // 2, 1), pltpu.roll(t, HEAD // 2, 1))
        return t * cos + rot * sin

    d = q_ref.shape[1]
    kv = k_ref.shape[1]
    q = _dot(hn, w_ref[:, 0:d]) + b_ref[:, 0:d]
    for c in range(d // 128):
        sl = slice(c * 128, (c + 1) * 128)
        q_ref[:, sl] = (rope(q[:, sl]) * (HEAD ** -0.5)).astype(BF16)
    k = _dot(hn, w_ref[:, d:d + kv]) + b_ref[:, d:d + kv]
    for c in range(kv // 128):
        sl = slice(c * 128, (c + 1) * 128)
        k_ref[:, sl] = rope(k[:, sl]).astype(BF16)
    v = _dot(hn, w_ref[:, d + kv:d + 2 * kv]) + b_ref[:, d + kv:d + 2 * kv]
    v_ref[...] = v.astype(BF16)
    z = _dot(hn, w_ref[:, d + 2 * kv:]) + b_ref[:, d + 2 * kv:]
    z_ref[...] = z * jax.nn.sigmoid(z)


def _swa_kernel(sink_ref, q_ref, kc_ref, kp_ref, vc_ref, vp_ref, zg_ref, o_ref):
    n = pl.program_id(1)
    blocks = q_ref.shape[0] // WINDOW
    si = lax.broadcasted_iota(jnp.int32, (2 * WINDOW, WINDOW), 0)
    qi = lax.broadcasted_iota(jnp.int32, (2 * WINDOW, WINDOW), 1)
    diff = qi + WINDOW - si
    band = (diff >= 0) & (diff < WINDOW)
    first_mask = band & ((si >= WINDOW) | (n > 0))
    kv_slices = [slice(kh * HEAD, (kh + 1) * HEAD) for kh in range(N_KV)]
    units = [(j, h) for j in range(blocks) for h in range(N_KV * GQA)]

    def window(cur_ref, prev_ref, j, ks):
        if j == 0:
            return jnp.concatenate([prev_ref[:, ks], cur_ref[0:WINDOW, ks]], axis=0)
        return cur_ref[(j - 1) * WINDOW:(j + 1) * WINDOW, ks]

    kcat = [[window(kc_ref, kp_ref, j, ks) for ks in kv_slices] for j in range(blocks)]
    vcat_t = [[window(vc_ref, vp_ref, j, ks).T for ks in kv_slices] for j in range(blocks)]
    s = [_dot_nt(kcat[j][h // GQA], q_ref[j * WINDOW:(j + 1) * WINDOW, h * HEAD:(h + 1) * HEAD])
         for j, h in units]
    s = [jnp.where(first_mask if j == 0 else band, x, MASK_VALUE) for (j, h), x in zip(units, s)]
    m = [jnp.maximum(jnp.max(x, axis=0, keepdims=True), sink_ref[h]) for (j, h), x in zip(units, s)]
    p = [jnp.exp(x - m_) for x, m_ in zip(s, m)]
    l = [jnp.sum(x, axis=0, keepdims=True) + jnp.exp(sink_ref[h] - m_)
         for (j, h), x, m_ in zip(units, p, m)]
    o_t = [_dot(vcat_t[j][h // GQA], x.astype(BF16)) / l_
           for (j, h), x, l_ in zip(units, p, l)]
    nh = N_KV * GQA
    for j in range(blocks):
        rows = slice(j * WINDOW, (j + 1) * WINDOW)
        o = jnp.concatenate(o_t[j * nh:(j + 1) * nh], axis=0).T
        o_ref[rows, :] = (o * zg_ref[rows, :]).astype(BF16)


def _out_ple_kernel(og_ref, h_ref, p_ref, wo_ref, wg_ref, wp_ref, g_ref, *out_refs, emit_h):
    h1 = h_ref[...] + _dot(og_ref[...], wo_ref[...])
    gate = jax.nn.sigmoid(_dot(h1.astype(BF16), wg_ref[...]))
    h2 = h1 + gate * _dot(p_ref[...].astype(BF16), wp_ref[...])
    if emit_h:
        out_refs[0][...] = h2
    out_refs[-1][...] = _rms(h2, g_ref[...])


def _rwkv_in_kernel(hn_ref, prev_ref, mu_ref, w_ref, w0_ref, w1_ref, w2_ref, a0_ref, a1_ref,
                    a2_ref, kk_ref, ka_ref, r_out, k_out, v_out, z_out, kk_out, b_out, ld_out,
                    *, tiles_per_seq):
    i = pl.program_id(0)
    hn = hn_ref[...]
    tm, d = hn.shape
    prev_row = jnp.where(i % tiles_per_seq == 0, 0.0, prev_ref[7:8, :])
    row = lax.broadcasted_iota(jnp.int32, (tm, d), 0)
    shifted = jnp.where(row == 0, prev_row, pltpu.roll(hn, 1, 0))
    xx = shifted - hn

    def lerp(c):
        return (hn + xx * mu_ref[c:c + 1, :]).astype(BF16)

    r_out[...] = _dot(lerp(0), w_ref[:, 0:d])
    k = _dot(lerp(1), w_ref[:, d:2 * d])
    v_out[...] = _dot(lerp(2), w_ref[:, 2 * d:3 * d])
    z_out[...] = _dot(lerp(3), w_ref[:, 3 * d:4 * d])

    lw = jnp.tanh(_dot(lerp(4), w1_ref[...])).astype(BF16)
    ld_out[...] = -DECAY_SCALE * jax.nn.sigmoid(w0_ref[...] + _dot(lw, w2_ref[...]))

    la = _dot(lerp(5), a1_ref[...]).astype(BF16)
    a = jax.nn.sigmoid(a0_ref[...] + _dot(la, a2_ref[...]))

    ones_bd = jnp.where(_block_ones(), 1.0, 0.0).astype(BF16)
    kk = k * kk_ref[...]
    for g in range(d // GROUP):
        sl = slice(g * GROUP, (g + 1) * GROUP)
        kg = kk[:, sl]
        nrm = jnp.sqrt(_seg_sum(kg * kg, ones_bd))
        kg = kg / jnp.maximum(nrm, 1e-12)
        kk_out[:, sl] = kg
        b_out[:, sl] = kg * a[:, sl]
    k_out[...] = k * (1.0 + (a - 1.0) * ka_ref[...])


def _scan_masks():
    t = lax.broadcasted_iota(jnp.int32, (CHUNK, GROUP), 0)
    s = lax.broadcasted_iota(jnp.int32, (CHUNK, GROUP), 1) & (HEAD - 1)
    same = lambda sh: (t >> sh) == (s >> sh)
    return dict(strict=s < t, incl=s <= t, eye=s == t, b8=same(3), b16=same(4), b32=same(5))


def _rwkv_scan_kernel(r_ref, k_ref, v_ref, kk_ref, b_ref, ld_ref, z_ref, rk_ref, gg_ref, gb_ref,
                      o_ref, s_ref):
    @pl.when(pl.program_id(1) == 0)
    def _():
        s_ref[...] = jnp.zeros_like(s_ref)

    rows, _, d = r_ref.shape
    lanes = lambda g: slice(g * GROUP, (g + 1) * GROUP)
    bdmask = _block_ones()
    ones_bd = jnp.where(bdmask, 1.0, 0.0).astype(BF16)
    mk = _scan_masks()

    def bd(x):
        return jnp.where(bdmask, jnp.tile(x.astype(BF16), (GROUP // HEAD, 1)), 0)

    def pmm(a_list, b_list):
        return [_dot(a.astype(BF16), bd(b)) for a, b in zip(a_list, b_list)]

    def add(a_list, b_list):
        return [a + b for a, b in zip(a_list, b_list)]

    def masked(name, x_list):
        return [jnp.where(mk[name], x, 0.0) for x in x_list]

    ti = lax.broadcasted_iota(jnp.int32, (CHUNK, 3 * CHUNK), 0)
    tj = lax.broadcasted_iota(jnp.int32, (CHUNK, 3 * CHUNK), 1) & (CHUNK - 1)
    tri3 = jnp.where(tj <= ti, 1.0, 0.0).astype(BF16)

    names = ("r", "k", "v", "kk", "b", "ld", "z")
    refs = (r_ref, k_ref, v_ref, kk_ref, b_ref, ld_ref, z_ref)
    inp = [{nm_: ref[i] for nm_, ref in zip(names, refs)} for i in range(rows)]
    state = {(i, g): s_ref[i, g] for i in range(rows) for g in range(d // GROUP)}
    rk, gg, gb = rk_ref[...], gg_ref[...], gb_ref[...]

    units = [(i, g) for i in range(rows) for g in range(d // GROUP)]
    each = lambda f: [f(i, g) for i, g in units]
    pre = []
    for i in range(rows):
        ld = inp[i]["ld"]
        c = _dot(tri3, jnp.concatenate(_split_bf16(ld, 3), axis=0))
        c_last = c[CHUNK - 1:CHUNK, :]
        r, k, b = inp[i]["r"], inp[i]["k"], inp[i]["b"]
        p_inv = jnp.exp(-c)
        p_rest = jnp.exp(c_last - c)
        pre.append(dict(
            r=r, k=k, v=inp[i]["v"], rt=r * jnp.exp(c), at=-inp[i]["kk"] * jnp.exp(c - ld),
            kt=k * p_inv, bt=b * p_inv, kh=k * p_rest, bh=b * p_rest, p_all=jnp.exp(c_last)))
    get = lambda name: each(lambda i, g: pre[i][name][:, lanes(g)])

    s0 = each(lambda i, g: state[i, g])
    x = each(lambda i, g: jnp.concatenate(
        [pre[i]["at"][:, lanes(g)], pre[i]["rt"][:, lanes(g)]], axis=0).astype(BF16))
    abk = [_dot_nt(xi, jnp.concatenate([bd(bt_), bd(kt_), s_.astype(BF16)], axis=0))
           for xi, bt_, kt_, s_ in zip(x, get("bt"), get("kt"), s0)]
    ab = [m[:, :GROUP] for m in abk]
    ak = [m[:, GROUP:2 * GROUP] for m in abk]
    gs = [m[:, 2 * GROUP:] for m in abk]

    nm = masked("strict", [m[:CHUNK] for m in ab])
    nd = masked("b8", nm)
    n2 = pmm(nd, nd)
    eye = jnp.where(mk["eye"], 1.0, 0.0)
    q = [eye + x_ for x_ in nd]
    nq = pmm([jnp.concatenate([n_, q_], axis=0) for n_, q_ in zip(n2, q)], n2)
    q = [q_ + m[CHUNK:] for q_, m in zip(q, nq)]
    q = add(q, pmm(q, [m[:CHUNK] for m in nq]))
    nm_bd = [bd(x_) for x_ in nm]
    for lo, hi in (("b8", "b16"), ("b16", "b32"), ("b32", None)):
        off = ~mk[lo] if hi is None else (mk[hi] & ~mk[lo])
        tn = [jnp.where(off, _dot(t.astype(BF16), w_), 0.0) for t, w_ in zip(q, nm_bd)]
        q = add(q, pmm(tn, q))

    a_rb = masked("incl", [m[CHUNK:] for m in ab])
    a_k = [jnp.concatenate([jnp.where(mk["strict"], m[:CHUNK], 0.0),
                            jnp.where(mk["incl"], m[CHUNK:], 0.0)], axis=0) for m in ak]
    vg = get("v")
    av = pmm(a_k, vg)
    w = [g_[:CHUNK] + a_[:CHUNK] for g_, a_ in zip(gs, av)]
    u = pmm(q, w)
    y = [g_[CHUNK:] + a_[CHUNK:] + c_ for g_, a_, c_ in zip(gs, av, pmm(a_rb, u))]
    ds = [_dot_tn(jnp.concatenate([v_, u_], axis=0).astype(BF16),
                  jnp.concatenate([kh_, bh_], axis=0).astype(BF16))
          for v_, u_, kh_, bh_ in zip(vg, u, get("kh"), get("bh"))]
    new_state = [jnp.where(bdmask, s_ * pre[i]["p_all"][:, lanes(g)] + ds_, 0.0)
                 for (i, g), s_, ds_ in zip(units, s0, ds)]

    rkr = each(lambda i, g: pre[i]["r"][:, lanes(g)] * pre[i]["k"][:, lanes(g)] * rk[:, lanes(g)])
    st1 = [_dot(jnp.concatenate(_split_bf16(y_, 2) + _split_bf16(q_, 2), axis=0), ones_bd)
           for y_, q_ in zip(y, rkr)]
    mean = [(s_[:CHUNK] + s_[CHUNK:2 * CHUNK]) * (1.0 / HEAD) for s_ in st1]
    yc = [y_ - m_ for y_, m_ in zip(y, mean)]
    st2 = [_dot(jnp.concatenate(_split_bf16(c_ * c_, 2), axis=0), ones_bd) for c_ in yc]
    outs = []
    for (i, g), yc_, s1, s2, v_ in zip(units, yc, st1, st2, vg):
        var = (s2[:CHUNK] + s2[CHUNK:]) * (1.0 / HEAD)
        yn = yc_ * lax.rsqrt(var + GN_EPS) * gg[:, lanes(g)] + gb[:, lanes(g)]
        bonus = (s1[2 * CHUNK:3 * CHUNK] + s1[3 * CHUNK:]) * v_
        z = inp[i]["z"][:, lanes(g)]
        outs.append(((yn + bonus) * (z * jax.nn.sigmoid(z))).astype(BF16))
    for (i, g), s_, o_ in zip(units, new_state, outs):
        s_ref[i, g] = s_
        o_ref[i, :, lanes(g)] = o_


def _row_spec(tm, width):
    return pl.BlockSpec((tm, width), lambda i: (i, 0))


def _full_spec(shape):
    return pl.BlockSpec(shape, lambda *_: (0,) * len(shape))


def _rope_tables(seq):
    half = HEAD // 2
    inv = ROPE_THETA ** (-jnp.arange(half, dtype=F32) / half)
    ang = jnp.arange(seq).astype(F32)[:, None] * inv[None, :]
    cos = jnp.cos(ang)
    sin = jnp.sin(ang)
    cos = jnp.tile(jnp.concatenate([cos, cos], axis=-1), (1, 128 // HEAD))
    sin = jnp.tile(jnp.concatenate([-sin, sin], axis=-1), (1, 128 // HEAD))
    return cos, sin


def _attn_in(x2, g, w, bias, seq, tm):
    n, d = x2.shape
    kv = N_KV * HEAD
    cos, sin = _rope_tables(seq)
    per_seq = seq // tm
    tab_spec = pl.BlockSpec((tm, 128), lambda i: (i % per_seq, 0))
    return pl.pallas_call(
        _attn_in_kernel,
        grid=(n // tm,),
        in_specs=[_row_spec(tm, d), _full_spec((1, d)), _full_spec(w.shape),
                  _full_spec((1, w.shape[1])), tab_spec, tab_spec],
        out_specs=[_row_spec(tm, d), _row_spec(tm, kv), _row_spec(tm, kv), _row_spec(tm, d)],
        out_shape=[jax.ShapeDtypeStruct((n, d), BF16), jax.ShapeDtypeStruct((n, kv), BF16),
                   jax.ShapeDtypeStruct((n, kv), BF16), jax.ShapeDtypeStruct((n, d), F32)],
        compiler_params=_cparams(1),
        name="attn_in",
    )(x2, g, w, bias, cos, sin)


def _swa(q, k, v, z, sinks, batch, seq):
    n, d = q.shape
    kv = k.shape[1]
    blocks = max(b_ for b_ in range(1, SWA_BLOCKS + 1) if (seq // WINDOW) % b_ == 0)
    nb = seq // (blocks * WINDOW)
    cur = lambda width: pl.BlockSpec((blocks * WINDOW, width), lambda b, j: (b * nb + j, 0))
    prev = lambda width: pl.BlockSpec(
        (WINDOW, width), lambda b, j: ((b * nb + j) * blocks - jnp.minimum(j, 1), 0))
    return pl.pallas_call(
        _swa_kernel,
        grid=(batch, nb),
        in_specs=[pl.BlockSpec(memory_space=pltpu.SMEM), cur(d), cur(kv), prev(kv), cur(kv),
                  prev(kv), cur(d)],
        out_specs=cur(d),
        out_shape=jax.ShapeDtypeStruct((n, d), BF16),
        compiler_params=_cparams(2),
        name="swa",
    )(sinks, q, k, k, v, v, z)


def _out_ple(og, h, p3, layer, wo, wg, wp, g, tm, emit_h):
    n, d = h.shape
    out_shape = [jax.ShapeDtypeStruct((n, d), F32)] * (2 if emit_h else 1)
    out_specs = [_row_spec(tm, d)] * (2 if emit_h else 1)
    return pl.pallas_call(
        functools.partial(_out_ple_kernel, emit_h=emit_h),
        grid=(n // tm,),
        in_specs=[_row_spec(tm, d), _row_spec(tm, d),
                  pl.BlockSpec((None, tm, p3.shape[2]), lambda i: (layer, i, 0)),
                  _full_spec(wo.shape), _full_spec(wg.shape), _full_spec(wp.shape),
                  _full_spec((1, d))],
        out_specs=out_specs,
        out_shape=out_shape,
        compiler_params=_cparams(1),
        name="out_ple_mid" if emit_h else "out_ple_last",
    )(og, h, p3, wo, wg, wp, g)


def _rwkv_in(hn, mu, w, w0, w1, w2, a0, a1, a2, k_k, k_a, seq, tm):
    n, d = hn.shape
    prev_spec = pl.BlockSpec((8, d), lambda i: (jnp.maximum(i * (tm // 8) - 1, 0), 0))
    vec = _full_spec((1, d))
    outs = 7
    return pl.pallas_call(
        functools.partial(_rwkv_in_kernel, tiles_per_seq=seq // tm),
        grid=(n // tm,),
        in_specs=[_row_spec(tm, d), prev_spec, _full_spec(mu.shape), _full_spec(w.shape), vec,
                  _full_spec(w1.shape), _full_spec(w2.shape), vec, _full_spec(a1.shape),
                  _full_spec(a2.shape), vec, vec],
        out_specs=[_row_spec(tm, d)] * outs,
        out_shape=[jax.ShapeDtypeStruct((n, d), F32)] * outs,
        compiler_params=_cparams(1),
        name="rwkv_in",
    )(hn, hn, mu, w, w0, w1, w2, a0, a1, a2, k_k, k_a)


def _rwkv_scan(r, k, v, kk, b, ld, z, r_k, gn_g, gn_b, batch, seq, rows):
    n, d = r.shape
    nc = seq // CHUNK
    blk = pl.BlockSpec((rows, CHUNK, d), lambda bi, ci: (bi, ci, 0))
    vec = pl.BlockSpec((1, d), lambda bi, ci: (0, 0))
    seqs = [a.reshape(batch, seq, d) for a in (r, k, v, kk, b, ld, z)]
    out = pl.pallas_call(
        _rwkv_scan_kernel,
        grid=(batch // rows, nc),
        in_specs=[blk] * 7 + [vec] * 3,
        out_specs=blk,
        out_shape=jax.ShapeDtypeStruct((batch, seq, d), BF16),
        scratch_shapes=[pltpu.VMEM((rows, d // GROUP, GROUP, GROUP), F32)],
        compiler_params=_cparams(2),
        name="rwkv_scan",
    )(*seqs, r_k, gn_g, gn_b)
    return out.reshape(n, d)


def kernel(x, p, norm_g, attn_w_in, attn_b_in, attn_sinks, attn_w_out, rwkv_mu, rwkv_w_in,
           rwkv_w0, rwkv_w1, rwkv_w2, rwkv_a0, rwkv_a1, rwkv_a2, rwkv_k_k, rwkv_k_a, rwkv_r_k,
           rwkv_gn_g, rwkv_gn_b, rwkv_w_out, ple_w_proj, ple_w_gate, final_norm_g):
    batch, seq, d = x.shape
    depth = p.shape[0]
    assert depth == 2 and d % GROUP == 0 and seq % WINDOW == 0
    n = batch * seq
    tm = min(512, seq)
    tm_r = min(256, seq)
    row = lambda a: a.reshape(1, -1).astype(F32)
    bf = lambda a: a.astype(BF16)

    h = x.reshape(n, d)
    p3 = p.reshape(depth, n, p.shape[-1])

    q, k, v, z = _attn_in(h, row(norm_g[0]), bf(attn_w_in[0]), row(attn_b_in[0]), seq, tm)
    og = _swa(q, k, v, z, attn_sinks[0].astype(F32), batch, seq)
    h, hn = _out_ple(og, h, p3, 0,bf(attn_w_out[0]), bf(ple_w_gate[0]), bf(ple_w_proj[0]),
                     row(norm_g[1]), tm, emit_h=True)

    r, k, v, z, kk, b, ld = _rwkv_in(
        hn, rwkv_mu[0].astype(F32), bf(rwkv_w_in[0]), row(rwkv_w0[0]), bf(rwkv_w1[0]),
        bf(rwkv_w2[0]), row(rwkv_a0[0]), bf(rwkv_a1[0]), bf(rwkv_a2[0]), row(rwkv_k_k[0]),
        row(rwkv_k_a[0]), seq, tm_r)
    yg = _rwkv_scan(r, k, v, kk, b, ld, z, row(rwkv_r_k[0]), row(rwkv_gn_g[0]),
                    row(rwkv_gn_b[0]), batch, seq,
                    rows=max(r_ for r_ in range(1, SCAN_ROWS + 1) if batch % r_ == 0))
    (out,) = _out_ple(yg, h, p3, 1,bf(rwkv_w_out[0]), bf(ple_w_gate[1]), bf(ple_w_proj[1]),
                      row(final_norm_g), tm, emit_h=False)
    return out.reshape(batch, seq, d)
```

```python
import functools
import math

import jax
import jax.numpy as jnp
from jax import lax
from jax.experimental import pallas as pl
from jax.experimental.pallas import tpu as pltpu

F32 = jnp.float32
BF16 = jnp.bfloat16

HEAD = 64
N_KV = 4
GQA = 4
WINDOW = 128
ROPE_THETA = 10000.0
MASK_VALUE = -1e30
NORM_EPS = 1e-6
GN_EPS = 64e-5
DECAY_SCALE = math.exp(-0.5)
CHUNK = 64
GROUP = 128
SCAN_ROWS = 8
SWA_BLOCKS = 4
VMEM_LIMIT = 56 * 1024 * 1024


def _cparams(n_axes):
    return pltpu.CompilerParams(
        dimension_semantics=("arbitrary",) * n_axes, vmem_limit_bytes=VMEM_LIMIT)


def _rms(x, g):
    ms = jnp.mean(x * x, axis=-1, keepdims=True)
    return x * lax.rsqrt(ms + NORM_EPS) * g


def _split_bf16(x, terms):
    parts = []
    rem = x
    for _ in range(terms):
        p = rem.astype(BF16)
        parts.append(p)
        rem = rem - p.astype(F32)
    return parts


def _dot(a, b):
    return jnp.dot(a, b, preferred_element_type=F32)


def _dot_nt(a, b):
    return lax.dot_general(a, b, (((1,), (1,)), ((), ())), preferred_element_type=F32)


def _dot_tn(a, b):
    return lax.dot_general(a, b, (((0,), (0,)), ((), ())), preferred_element_type=F32)


def _seg_sum(x, ones_bd):
    hi, lo = _split_bf16(x, 2)
    return _dot(hi, ones_bd) + _dot(lo, ones_bd)


def _block_ones():
    r = lax.broadcasted_iota(jnp.int32, (GROUP, GROUP), 0) >> 6
    c = lax.broadcasted_iota(jnp.int32, (GROUP, GROUP), 1) >> 6
    return r == c


def _attn_in_kernel(x_ref, g_ref, w_ref, b_ref, cos_ref, sin_ref, q_ref, k_ref, v_ref, z_ref):
    hn = _rms(x_ref[...], g_ref[...]).astype(BF16)
    cos = cos_ref[...]
    sin = sin_ref[...]
    first = (lax.broadcasted_iota(jnp.int32, cos.shape, 1) & (HEAD - 1)) < HEAD // 2

    def rope(t):
        rot = jnp.where(first, pltpu.roll(t, 128 - HEAD // 2, 1), pltpu.roll(t, HEAD // 2, 1))
        return t * cos + rot * sin

    d = q_ref.shape[1]
    kv = k_ref.shape[1]
    q = _dot(hn, w_ref[:, 0:d]) + b_ref[:, 0:d]
    for c in range(d // 128):
        sl = slice(c * 128, (c + 1) * 128)
        q_ref[:, sl] = (rope(q[:, sl]) * (HEAD ** -0.5)).astype(BF16)
    k = _dot(hn, w_ref[:, d:d + kv]) + b_ref[:, d:d + kv]
    for c in range(kv // 128):
        sl = slice(c * 128, (c + 1) * 128)
        k_ref[:, sl] = rope(k[:, sl]).astype(BF16)
    v = _dot(hn, w_ref[:, d + kv:d + 2 * kv]) + b_ref[:, d + kv:d + 2 * kv]
    v_ref[...] = v.astype(BF16)
    z = _dot(hn, w_ref[:, d + 2 * kv:]) + b_ref[:, d + 2 * kv:]
    z_ref[...] = z * jax.nn.sigmoid(z)


def _swa_kernel(sink_ref, q_ref, kc_ref, kp_ref, vc_ref, vp_ref, zg_ref, o_ref):
    n = pl.program_id(1)
    blocks = q_ref.shape[0] // WINDOW
    si = lax.broadcasted_iota(jnp.int32, (2 * WINDOW, WINDOW), 0)
    qi = lax.broadcasted_iota(jnp.int32, (2 * WINDOW, WINDOW), 1)
    diff = qi + WINDOW - si
    band = (diff >= 0) & (diff < WINDOW)
    first_mask = band & ((si >= WINDOW) | (n > 0))
    kv_slices = [slice(kh * HEAD, (kh + 1) * HEAD) for kh in range(N_KV)]
    units = [(j, h) for j in range(blocks) for h in range(N_KV * GQA)]

    def window(cur_ref, prev_ref, j, ks):
        if j == 0:
            return jnp.concatenate([prev_ref[:, ks], cur_ref[0:WINDOW, ks]], axis=0)
        return cur_ref[(j - 1) * WINDOW:(j + 1) * WINDOW, ks]

    kcat = [[window(kc_ref, kp_ref, j, ks) for ks in kv_slices] for j in range(blocks)]
    vcat_t = [[window(vc_ref, vp_ref, j, ks).T for ks in kv_slices] for j in range(blocks)]
    s = [_dot_nt(kcat[j][h // GQA], q_ref[j * WINDOW:(j + 1) * WINDOW, h * HEAD:(h + 1) * HEAD])
         for j, h in units]
    s = [jnp.where(first_mask if j == 0 else band, x, MASK_VALUE) for (j, h), x in zip(units, s)]
    m = [jnp.maximum(jnp.max(x, axis=0, keepdims=True), sink_ref[h]) for (j, h), x in zip(units, s)]
    p = [jnp.exp(x - m_) for x, m_ in zip(s, m)]
    l = [jnp.sum(x, axis=0, keepdims=True) + jnp.exp(sink_ref[h] - m_)
         for (j, h), x, m_ in zip(units, p, m)]
    o_t = [_dot(vcat_t[j][h // GQA], x.astype(BF16)) / l_
           for (j, h), x, l_ in zip(units, p, l)]
    nh = N_KV * GQA
    for j in range(blocks):
        rows = slice(j * WINDOW, (j + 1) * WINDOW)
        o = jnp.concatenate(o_t[j * nh:(j + 1) * nh], axis=0).T
        o_ref[rows, :] = (o * zg_ref[rows, :]).astype(BF16)


def _out_ple_kernel(og_ref, h_ref, p_ref, wo_ref, wg_ref, wp_ref, g_ref, *out_refs, emit_h):
    h1 = h_ref[...] + _dot(og_ref[...], wo_ref[...])
    gate = jax.nn.sigmoid(_dot(h1.astype(BF16), wg_ref[...]))
    h2 = h1 + gate * _dot(p_ref[...].astype(BF16), wp_ref[...])
    if emit_h:
        out_refs[0][...] = h2
    out_refs[-1][...] = _rms(h2, g_ref[...])


def _rwkv_in_kernel(hn_ref, prev_ref, mu_ref, w_ref, w0_ref, w1_ref, w2_ref, a0_ref, a1_ref,
                    a2_ref, kk_ref, ka_ref, r_out, k_out, v_out, z_out, kk_out, b_out, ld_out,
                    *, tiles_per_seq):
    i = pl.program_id(0)
    hn = hn_ref[...]
    tm, d = hn.shape
    prev_row = jnp.where(i % tiles_per_seq == 0, 0.0, prev_ref[7:8, :])
    row = lax.broadcasted_iota(jnp.int32, (tm, d), 0)
    shifted = jnp.where(row == 0, prev_row, pltpu.roll(hn, 1, 0))
    xx = shifted - hn

    def lerp(c):
        return (hn + xx * mu_ref[c:c + 1, :]).astype(BF16)

    r_out[...] = _dot(lerp(0), w_ref[:, 0:d])
    k = _dot(lerp(1), w_ref[:, d:2 * d])
    v_out[...] = _dot(lerp(2), w_ref[:, 2 * d:3 * d])
    z_out[...] = _dot(lerp(3), w_ref[:, 3 * d:4 * d])

    lw = jnp.tanh(_dot(lerp(4), w1_ref[...])).astype(BF16)
    ld_out[...] = -DECAY_SCALE * jax.nn.sigmoid(w0_ref[...] + _dot(lw, w2_ref[...]))

    la = _dot(lerp(5), a1_ref[...]).astype(BF16)
    a = jax.nn.sigmoid(a0_ref[...] + _dot(la, a2_ref[...]))

    ones_bd = jnp.where(_block_ones(), 1.0, 0.0).astype(BF16)
    kk = k * kk_ref[...]
    for g in range(d // GROUP):
        sl = slice(g * GROUP, (g + 1) * GROUP)
        kg = kk[:, sl]
        nrm = jnp.sqrt(_seg_sum(kg * kg, ones_bd))
        kg = kg / jnp.maximum(nrm, 1e-12)
        kk_out[:, sl] = kg
        b_out[:, sl] = kg * a[:, sl]
    k_out[...] = k * (1.0 + (a - 1.0) * ka_ref[...])


def _scan_masks():
    t = lax.broadcasted_iota(jnp.int32, (CHUNK, GROUP), 0)
    s = lax.broadcasted_iota(jnp.int32, (CHUNK, GROUP), 1) & (HEAD - 1)
    same = lambda sh: (t >> sh) == (s >> sh)
    return dict(strict=s < t, incl=s <= t, eye=s == t, b8=same(3), b16=same(4), b32=same(5))


def _rwkv_scan_kernel(r_ref, k_ref, v_ref, kk_ref, b_ref, ld_ref, z_ref, rk_ref, gg_ref, gb_ref,
                      o_ref, s_ref):
    @pl.when(pl.program_id(1) == 0)
    def _():
        s_ref[...] = jnp.zeros_like(s_ref)

    rows, _, d = r_ref.shape
    lanes = lambda g: slice(g * GROUP, (g + 1) * GROUP)
    bdmask = _block_ones()
    ones_bd = jnp.where(bdmask, 1.0, 0.0).astype(BF16)
    mk = _scan_masks()

    def bd(x):
        return jnp.where(bdmask, jnp.tile(x.astype(BF16), (GROUP // HEAD, 1)), 0)

    def pmm(a_list, b_list):
        return [_dot(a.astype(BF16), bd(b)) for a, b in zip(a_list, b_list)]

    def add(a_list, b_list):
        return [a + b for a, b in zip(a_list, b_list)]

    def masked(name, x_list):
        return [jnp.where(mk[name], x, 0.0) for x in x_list]

    ti = lax.broadcasted_iota(jnp.int32, (CHUNK, 3 * CHUNK), 0)
    tj = lax.broadcasted_iota(jnp.int32, (CHUNK, 3 * CHUNK), 1) & (CHUNK - 1)
    tri3 = jnp.where(tj <= ti, 1.0, 0.0).astype(BF16)

    names = ("r", "k", "v", "kk", "b", "ld", "z")
    refs = (r_ref, k_ref, v_ref, kk_ref, b_ref, ld_ref, z_ref)
    inp = [{nm_: ref[i] for nm_, ref in zip(names, refs)} for i in range(rows)]
    state = {(i, g): s_ref[i, g] for i in range(rows) for g in range(d // GROUP)}
    rk, gg, gb = rk_ref[...], gg_ref[...], gb_ref[...]

    units = [(i, g) for i in range(rows) for g in range(d // GROUP)]
    each = lambda f: [f(i, g) for i, g in units]
    pre = []
    for i in range(rows):
        ld = inp[i]["ld"]
        c = _dot(tri3, jnp.concatenate(_split_bf16(ld, 3), axis=0))
        c_last = c[CHUNK - 1:CHUNK, :]
        r, k, b = inp[i]["r"], inp[i]["k"], inp[i]["b"]
        p_inv = jnp.exp(-c)
        p_rest = jnp.exp(c_last - c)
        pre.append(dict(
            r=r, k=k, v=inp[i]["v"], rt=r * jnp.exp(c), at=-inp[i]["kk"] * jnp.exp(c - ld),
            kt=k * p_inv, bt=b * p_inv, kh=k * p_rest, bh=b * p_rest, p_all=jnp.exp(c_last)))
    get = lambda name: each(lambda i, g: pre[i][name][:, lanes(g)])

    s0 = each(lambda i, g: state[i, g])
    x = each(lambda i, g: jnp.concatenate(
        [pre[i]["at"][:, lanes(g)], pre[i]["rt"][:, lanes(g)]], axis=0).astype(BF16))
    abk = [_dot_nt(xi, jnp.concatenate([bd(bt_), bd(kt_), s_.astype(BF16)], axis=0))
           for xi, bt_, kt_, s_ in zip(x, get("bt"), get("kt"), s0)]
    ab = [m[:, :GROUP] for m in abk]
    ak = [m[:, GROUP:2 * GROUP] for m in abk]
    gs = [m[:, 2 * GROUP:] for m in abk]

    nm = masked("strict", [m[:CHUNK] for m in ab])
    nd = masked("b8", nm)
    n2 = pmm(nd, nd)
    eye = jnp.where(mk["eye"], 1.0, 0.0)
    q = [eye + x_ for x_ in nd]
    nq = pmm([jnp.concatenate([n_, q_], axis=0) for n_, q_ in zip(n2, q)], n2)
    q = [q_ + m[CHUNK:] for q_, m in zip(q, nq)]
    q = add(q, pmm(q, [m[:CHUNK] for m in nq]))
    nm_bd = [bd(x_) for x_ in nm]
    for lo, hi in (("b8", "b16"), ("b16", "b32"), ("b32", None)):
        off = ~mk[lo] if hi is None else (mk[hi] & ~mk[lo])
        tn = [jnp.where(off, _dot(t.astype(BF16), w_), 0.0) for t, w_ in zip(q, nm_bd)]
        q = add(q, pmm(tn, q))

    a_rb = masked("incl", [m[CHUNK:] for m in ab])
    a_k = [jnp.concatenate([jnp.where(mk["strict"], m[:CHUNK], 0.0),
                            jnp.where(mk["incl"], m[CHUNK:], 0.0)], axis=0) for m in ak]
    vg = get("v")
    av = pmm(a_k, vg)
    w = [g_[:CHUNK] + a_[:CHUNK] for g_, a_ in zip(gs, av)]
    u = pmm(q, w)
    y = [g_[CHUNK:] + a_[CHUNK:] + c_ for g_, a_, c_ in zip(gs, av, pmm(a_rb, u))]
    ds = [_dot_tn(jnp.concatenate([v_, u_], axis=0).astype(BF16),
                  jnp.concatenate([kh_, bh_], axis=0).astype(BF16))
          for v_, u_, kh_, bh_ in zip(vg, u, get("kh"), get("bh"))]
    new_state = [jnp.where(bdmask, s_ * pre[i]["p_all"][:, lanes(g)] + ds_, 0.0)
                 for (i, g), s_, ds_ in zip(units, s0, ds)]

    rkr = each(lambda i, g: pre[i]["r"][:, lanes(g)] * pre[i]["k"][:, lanes(g)] * rk[:, lanes(g)])
    st1 = [_dot(jnp.concatenate(_split_bf16(y_, 2) + _split_bf16(q_, 2), axis=0), ones_bd)
           for y_, q_ in zip(y, rkr)]
    mean = [(s_[:CHUNK] + s_[CHUNK:2 * CHUNK]) * (1.0 / HEAD) for s_ in st1]
    yc = [y_ - m_ for y_, m_ in zip(y, mean)]
    st2 = [_dot(jnp.concatenate(_split_bf16(c_ * c_, 2), axis=0), ones_bd) for c_ in yc]
    outs = []
    for (i, g), yc_, s1, s2, v_ in zip(units, yc, st1, st2, vg):
        var = (s2[:CHUNK] + s2[CHUNK:]) * (1.0 / HEAD)
        yn = yc_ * lax.rsqrt(var + GN_EPS) * gg[:, lanes(g)] + gb[:, lanes(g)]
        bonus = (s1[2 * CHUNK:3 * CHUNK] + s1[3 * CHUNK:]) * v_
        z = inp[i]["z"][:, lanes(g)]
        outs.append(((yn + bonus) * (z * jax.nn.sigmoid(z))).astype(BF16))
    for (i, g), s_, o_ in zip(units, new_state, outs):
        s_ref[i, g] = s_
        o_ref[i, :, lanes(g)] = o_


def _row_spec(tm, width):
    return pl.BlockSpec((tm, width), lambda i: (i, 0))


def _full_spec(shape):
    return pl.BlockSpec(shape, lambda *_: (0,) * len(shape))


def _rope_tables(seq):
    half = HEAD // 2
    inv = ROPE_THETA ** (-jnp.arange(half, dtype=F32) / half)
    ang = jnp.arange(seq).astype(F32)[:, None] * inv[None, :]
    cos = jnp.cos(ang)
    sin = jnp.sin(ang)
    cos = jnp.tile(jnp.concatenate([cos, cos], axis=-1), (1, 128 // HEAD))
    sin = jnp.tile(jnp.concatenate([-sin, sin], axis=-1), (1, 128 // HEAD))
    return cos, sin


def _attn_in(x2, g, w, bias, seq, tm):
    n, d = x2.shape
    kv = N_KV * HEAD
    cos, sin = _rope_tables(seq)
    per_seq = seq // tm
    tab_spec = pl.BlockSpec((tm, 128), lambda i: (i % per_seq, 0))
    return pl.pallas_call(
        _attn_in_kernel,
        grid=(n // tm,),
        in_specs=[_row_spec(tm, d), _full_spec((1, d)), _full_spec(w.shape),
                  _full_spec((1, w.shape[1])), tab_spec, tab_spec],
        out_specs=[_row_spec(tm, d), _row_spec(tm, kv), _row_spec(tm, kv), _row_spec(tm, d)],
        out_shape=[jax.ShapeDtypeStruct((n, d), BF16), jax.ShapeDtypeStruct((n, kv), BF16),
                   jax.ShapeDtypeStruct((n, kv), BF16), jax.ShapeDtypeStruct((n, d), F32)],
        compiler_params=_cparams(1),
        name="attn_in",
    )(x2, g, w, bias, cos, sin)


def _swa(q, k, v, z, sinks, batch, seq):
    n, d = q.shape
    kv = k.shape[1]
    blocks = max(b_ for b_ in range(1, SWA_BLOCKS + 1) if (seq // WINDOW) % b_ == 0)
    nb = seq // (blocks * WINDOW)
    cur = lambda width: pl.BlockSpec((blocks * WINDOW, width), lambda b, j: (b * nb + j, 0))
    prev = lambda width: pl.BlockSpec(
        (WINDOW, width), lambda b, j: ((b * nb + j) * blocks - jnp.minimum(j, 1), 0))
    return pl.pallas_call(
        _swa_kernel,
        grid=(batch, nb),
        in_specs=[pl.BlockSpec(memory_space=pltpu.SMEM), cur(d), cur(kv), prev(kv), cur(kv),
                  prev(kv), cur(d)],
        out_specs=cur(d),
        out_shape=jax.ShapeDtypeStruct((n, d), BF16),
        compiler_params=_cparams(2),
        name="swa",
    )(sinks, q, k, k, v, v, z)


def _out_ple(og, h, p3, layer, wo, wg, wp, g, tm, emit_h):
    n, d = h.shape
    out_shape = [jax.ShapeDtypeStruct((n, d), F32)] * (2 if emit_h else 1)
    out_specs = [_row_spec(tm, d)] * (2 if emit_h else 1)
    return pl.pallas_call(
        functools.partial(_out_ple_kernel, emit_h=emit_h),
        grid=(n // tm,),
        in_specs=[_row_spec(tm, d), _row_spec(tm, d),
                  pl.BlockSpec((None, tm, p3.shape[2]), lambda i: (layer, i, 0)),
                  _full_spec(wo.shape), _full_spec(wg.shape), _full_spec(wp.shape),
                  _full_spec((1, d))],
        out_specs=out_specs,
        out_shape=out_shape,
        compiler_params=_cparams(1),
        name="out_ple_mid" if emit_h else "out_ple_last",
    )(og, h, p3, wo, wg, wp, g)


def _rwkv_in(hn, mu, w, w0, w1, w2, a0, a1, a2, k_k, k_a, seq, tm):
    n, d = hn.shape
    prev_spec = pl.BlockSpec((8, d), lambda i: (jnp.maximum(i * (tm // 8) - 1, 0), 0))
    vec = _full_spec((1, d))
    outs = 7
    return pl.pallas_call(
        functools.partial(_rwkv_in_kernel, tiles_per_seq=seq // tm),
        grid=(n // tm,),
        in_specs=[_row_spec(tm, d), prev_spec, _full_spec(mu.shape), _full_spec(w.shape), vec,
                  _full_spec(w1.shape), _full_spec(w2.shape), vec, _full_spec(a1.shape),
                  _full_spec(a2.shape), vec, vec],
        out_specs=[_row_spec(tm, d)] * outs,
        out_shape=[jax.ShapeDtypeStruct((n, d), F32)] * outs,
        compiler_params=_cparams(1),
        name="rwkv_in",
    )(hn, hn, mu, w, w0, w1, w2, a0, a1, a2, k_k, k_a)


def _rwkv_scan(r, k, v, kk, b, ld, z, r_k, gn_g, gn_b, batch, seq, rows):
    n, d = r.shape
    nc = seq // CHUNK
    blk = pl.BlockSpec((rows, CHUNK, d), lambda bi, ci: (bi, ci, 0))
    vec = pl.BlockSpec((1, d), lambda bi, ci: (0, 0))
    seqs = [a.reshape(batch, seq, d) for a in (r, k, v, kk, b, ld, z)]
    out = pl.pallas_call(
        _rwkv_scan_kernel,
        grid=(batch // rows, nc),
        in_specs=[blk] * 7 + [vec] * 3,
        out_specs=blk,
        out_shape=jax.ShapeDtypeStruct((batch, seq, d), BF16),
        scratch_shapes=[pltpu.VMEM((rows, d // GROUP, GROUP, GROUP), F32)],
        compiler_params=_cparams(2),
        name="rwkv_scan",
    )(*seqs, r_k, gn_g, gn_b)
    return out.reshape(n, d)


def kernel(x, p, norm_g, attn_w_in, attn_b_in, attn_sinks, attn_w_out, rwkv_mu, rwkv_w_in,
           rwkv_w0, rwkv_w1, rwkv_w2, rwkv_a0, rwkv_a1, rwkv_a2, rwkv_k_k, rwkv_k_a, rwkv_r_k,
           rwkv_gn_g, rwkv_gn_b, rwkv_w_out, ple_w_proj, ple_w_gate, final_norm_g):
    batch, seq, d = x.shape
    depth = p.shape[0]
    assert depth == 2 and d % GROUP == 0 and seq % WINDOW == 0
    n = batch * seq
    tm = min(1024, seq)
    tm_r = min(512, seq)
    row = lambda a: a.reshape(1, -1).astype(F32)
    bf = lambda a: a.astype(BF16)

    h = x.reshape(n, d)
    p3 = p.reshape(depth, n, p.shape[-1])

    q, k, v, z = _attn_in(h, row(norm_g[0]), bf(attn_w_in[0]), row(attn_b_in[0]), seq, tm)
    og = _swa(q, k, v, z, attn_sinks[0].astype(F32), batch, seq)
    h, hn = _out_ple(og, h, p3, 0,bf(attn_w_out[0]), bf(ple_w_gate[0]), bf(ple_w_proj[0]),
                     row(norm_g[1]), tm, emit_h=True)

    r, k, v, z, kk, b, ld = _rwkv_in(
        hn, rwkv_mu[0].astype(F32), bf(rwkv_w_in[0]), row(rwkv_w0[0]), bf(rwkv_w1[0]),
        bf(rwkv_w2[0]), row(rwkv_a0[0]), bf(rwkv_a1[0]), bf(rwkv_a2[0]), row(rwkv_k_k[0]),
        row(rwkv_k_a[0]), seq, tm_r)
    yg = _rwkv_scan(r, k, v, kk, b, ld, z, row(rwkv_r_k[0]), row(rwkv_gn_g[0]),
                    row(rwkv_gn_b[0]), batch, seq,
                    rows=max(r_ for r_ in range(1, SCAN_ROWS + 1) if batch % r_ == 0))
    (out,) = _out_ple(yg, h, p3, 1,bf(rwkv_w_out[0]), bf(ple_w_gate[1]), bf(ple_w_proj[1]),
                      row(final_norm_g), tm, emit_h=False)
    return out.reshape(batch, seq, d)
```

```python
import functools
import math

import jax
import jax.numpy as jnp
from jax import lax
from jax.experimental import pallas as pl
from jax.experimental.pallas import tpu as pltpu

F32 = jnp.float32
BF16 = jnp.bfloat16

HEAD = 64
N_KV = 4
GQA = 4
WINDOW = 128
ROPE_THETA = 10000.0
MASK_VALUE = -1e30
NORM_EPS = 1e-6
GN_EPS = 64e-5
DECAY_SCALE = math.exp(-0.5)
CHUNK = 64
GROUP = 128
SCAN_ROWS = 8
SWA_BLOCKS = 4
VMEM_LIMIT = 56 * 1024 * 1024


def _cparams(n_axes):
    return pltpu.CompilerParams(
        dimension_semantics=("arbitrary",) * n_axes, vmem_limit_bytes=VMEM_LIMIT)


def _rms(x, g):
    ms = jnp.mean(x * x, axis=-1, keepdims=True)
    return x * lax.rsqrt(ms + NORM_EPS) * g


def _split_bf16(x, terms):
    parts = []
    rem = x
    for _ in range(terms):
        p = rem.astype(BF16)
        parts.append(p)
        rem = rem - p.astype(F32)
    return parts


def _dot(a, b):
    return jnp.dot(a, b, preferred_element_type=F32)


def _dot_nt(a, b):
    return lax.dot_general(a, b, (((1,), (1,)), ((), ())), preferred_element_type=F32)


def _dot_tn(a, b):
    return lax.dot_general(a, b, (((0,), (0,)), ((), ())), preferred_element_type=F32)


def _seg_sum(x, ones_bd):
    hi, lo = _split_bf16(x, 2)
    return _dot(hi, ones_bd) + _dot(lo, ones_bd)


def _block_ones():
    r = lax.broadcasted_iota(jnp.int32, (GROUP, GROUP), 0) >> 6
    c = lax.broadcasted_iota(jnp.int32, (GROUP, GROUP), 1) >> 6
    return r == c


def _attn_in_kernel(x_ref, g_ref, w_ref, b_ref, cos_ref, sin_ref, q_ref, k_ref, v_ref, z_ref):
    hn = _rms(x_ref[...], g_ref[...]).astype(BF16)
    cos = cos_ref[...]
    sin = sin_ref[...]
    first = (lax.broadcasted_iota(jnp.int32, cos.shape, 1) & (HEAD - 1)) < HEAD // 2

    def rope(t):
        rot = jnp.where(first, pltpu.roll(t, 128 - HEAD // 2, 1), pltpu.roll(t, HEAD // 2, 1))
        return t * cos + rot * sin

    d = q_ref.shape[1]
    kv = k_ref.shape[1]
    q = _dot(hn, w_ref[:, 0:d]) + b_ref[:, 0:d]
    for c in range(d // 128):
        sl = slice(c * 128, (c + 1) * 128)
        q_ref[:, sl] = (rope(q[:, sl]) * (HEAD ** -0.5)).astype(BF16)
    k = _dot(hn, w_ref[:, d:d + kv]) + b_ref[:, d:d + kv]
    for c in range(kv // 128):
        sl = slice(c * 128, (c + 1) * 128)
        k_ref[:, sl] = rope(k[:, sl]).astype(BF16)
    v = _dot(hn, w_ref[:, d + kv:d + 2 * kv]) + b_ref[:, d + kv:d + 2 * kv]
    v_ref[...] = v.astype(BF16)
    z = _dot(hn, w_ref[:, d + 2 * kv:]) + b_ref[:, d + 2 * kv:]
    z_ref[...] = z * jax.nn.sigmoid(z)


def _swa_kernel(sink_ref, q_ref, kc_ref, kp_ref, vc_ref, vp_ref, zg_ref, o_ref):
    n = pl.program_id(1)
    blocks = q_ref.shape[0] // WINDOW
    si = lax.broadcasted_iota(jnp.int32, (2 * WINDOW, WINDOW), 0)
    qi = lax.broadcasted_iota(jnp.int32, (2 * WINDOW, WINDOW), 1)
    diff = qi + WINDOW - si
    band = (diff >= 0) & (diff < WINDOW)
    first_mask = band & ((si >= WINDOW) | (n > 0))
    kv_slices = [slice(kh * HEAD, (kh + 1) * HEAD) for kh in range(N_KV)]
    units = [(j, h) for j in range(blocks) for h in range(N_KV * GQA)]

    def window(cur_ref, prev_ref, j, ks):
        if j == 0:
            return jnp.concatenate([prev_ref[:, ks], cur_ref[0:WINDOW, ks]], axis=0)
        return cur_ref[(j - 1) * WINDOW:(j + 1) * WINDOW, ks]

    kcat = [[window(kc_ref, kp_ref, j, ks) for ks in kv_slices] for j in range(blocks)]
    vcat_t = [[window(vc_ref, vp_ref, j, ks).T for ks in kv_slices] for j in range(blocks)]
    s = [_dot_nt(kcat[j][h // GQA], q_ref[j * WINDOW:(j + 1) * WINDOW, h * HEAD:(h + 1) * HEAD])
         for j, h in units]
    s = [jnp.where(first_mask if j == 0 else band, x, MASK_VALUE) for (j, h), x in zip(units, s)]
    m = [jnp.maximum(jnp.max(x, axis=0, keepdims=True), sink_ref[h]) for (j, h), x in zip(units, s)]
    p = [jnp.exp(x - m_) for x, m_ in zip(s, m)]
    l = [jnp.sum(x, axis=0, keepdims=True) + jnp.exp(sink_ref[h] - m_)
         for (j, h), x, m_ in zip(units, p, m)]
    o_t = [_dot(vcat_t[j][h // GQA], x.astype(BF16)) / l_
           for (j, h), x, l_ in zip(units, p, l)]
    nh = N_KV * GQA
    for j in range(blocks):
        rows = slice(j * WINDOW, (j + 1) * WINDOW)
        o = jnp.concatenate(o_t[j * nh:(j + 1) * nh], axis=0).T
        o_ref[rows, :] = (o * zg_ref[rows, :]).astype(BF16)


def _out_ple_kernel(og_ref, h_ref, p_ref, wo_ref, wg_ref, wp_ref, g_ref, *out_refs, emit_h):
    h1 = h_ref[...] + _dot(og_ref[...], wo_ref[...])
    gate = jax.nn.sigmoid(_dot(h1.astype(BF16), wg_ref[...]))
    h2 = h1 + gate * _dot(p_ref[...].astype(BF16), wp_ref[...])
    if emit_h:
        out_refs[0][...] = h2
    out_refs[-1][...] = _rms(h2, g_ref[...])


def _rwkv_in_kernel(hn_ref, prev_ref, mu_ref, w_ref, w0_ref, w1_ref, w2_ref, a0_ref, a1_ref,
                    a2_ref, kk_ref, ka_ref, r_out, k_out, v_out, z_out, kk_out, b_out, ld_out,
                    *, tiles_per_seq):
    i = pl.program_id(0)
    hn = hn_ref[...]
    tm, d = hn.shape
    prev_row = jnp.where(i % tiles_per_seq == 0, 0.0, prev_ref[7:8, :])
    row = lax.broadcasted_iota(jnp.int32, (tm, d), 0)
    shifted = jnp.where(row == 0, prev_row, pltpu.roll(hn, 1, 0))
    xx = shifted - hn

    def lerp(c):
        return (hn + xx * mu_ref[c:c + 1, :]).astype(BF16)

    r_out[...] = _dot(lerp(0), w_ref[:, 0:d])
    k = _dot(lerp(1), w_ref[:, d:2 * d])
    v_out[...] = _dot(lerp(2), w_ref[:, 2 * d:3 * d])
    z_out[...] = _dot(lerp(3), w_ref[:, 3 * d:4 * d])

    lw = jnp.tanh(_dot(lerp(4), w1_ref[...])).astype(BF16)
    ld_out[...] = -DECAY_SCALE * jax.nn.sigmoid(w0_ref[...] + _dot(lw, w2_ref[...]))

    la = _dot(lerp(5), a1_ref[...]).astype(BF16)
    a = jax.nn.sigmoid(a0_ref[...] + _dot(la, a2_ref[...]))

    ones_bd = jnp.where(_block_ones(), 1.0, 0.0).astype(BF16)
    kk = k * kk_ref[...]
    for g in range(d // GROUP):
        sl = slice(g * GROUP, (g + 1) * GROUP)
        kg = kk[:, sl]
        nrm = jnp.sqrt(_seg_sum(kg * kg, ones_bd))
        kg = kg / jnp.maximum(nrm, 1e-12)
        kk_out[:, sl] = kg
        b_out[:, sl] = kg * a[:, sl]
    k_out[...] = k * (1.0 + (a - 1.0) * ka_ref[...])


def _scan_masks():
    t = lax.broadcasted_iota(jnp.int32, (CHUNK, GROUP), 0)
    s = lax.broadcasted_iota(jnp.int32, (CHUNK, GROUP), 1) & (HEAD - 1)
    same = lambda sh: (t >> sh) == (s >> sh)
    return dict(strict=s < t, incl=s <= t, eye=s == t, b2=same(1), b4=same(2), b8=same(3),
                b16=same(4), b32=same(5))


def _rwkv_scan_kernel(r_ref, k_ref, v_ref, kk_ref, b_ref, ld_ref, z_ref, rk_ref, gg_ref, gb_ref,
                      o_ref, s_ref):
    @pl.when(pl.program_id(1) == 0)
    def _():
        s_ref[...] = jnp.zeros_like(s_ref)

    rows, _, d = r_ref.shape
    lanes = lambda g: slice(g * GROUP, (g + 1) * GROUP)
    bdmask = _block_ones()
    ones_bd = jnp.where(bdmask, 1.0, 0.0).astype(BF16)
    mk = _scan_masks()

    def bd(x):
        return jnp.where(bdmask, jnp.tile(x.astype(BF16), (GROUP // HEAD, 1)), 0)

    def pmm(a_list, b_list):
        return [_dot(a.astype(BF16), bd(b)) for a, b in zip(a_list, b_list)]

    def add(a_list, b_list):
        return [a + b for a, b in zip(a_list, b_list)]

    def masked(name, x_list):
        return [jnp.where(mk[name], x, 0.0) for x in x_list]

    ti = lax.broadcasted_iota(jnp.int32, (CHUNK, 3 * CHUNK), 0)
    tj = lax.broadcasted_iota(jnp.int32, (CHUNK, 3 * CHUNK), 1) & (CHUNK - 1)
    tri3 = jnp.where(tj <= ti, 1.0, 0.0).astype(BF16)

    names = ("r", "k", "v", "kk", "b", "ld", "z")
    refs = (r_ref, k_ref, v_ref, kk_ref, b_ref, ld_ref, z_ref)
    inp = [{nm_: ref[i] for nm_, ref in zip(names, refs)} for i in range(rows)]
    state = {(i, g): s_ref[i, g] for i in range(rows) for g in range(d // GROUP)}
    rk, gg, gb = rk_ref[...], gg_ref[...], gb_ref[...]

    units = [(i, g) for i in range(rows) for g in range(d // GROUP)]
    each = lambda f: [f(i, g) for i, g in units]
    pre = []
    for i in range(rows):
        ld = inp[i]["ld"]
        c = _dot(tri3, jnp.concatenate(_split_bf16(ld, 3), axis=0))
        c_last = c[CHUNK - 1:CHUNK, :]
        r, k, b = inp[i]["r"], inp[i]["k"], inp[i]["b"]
        p_inv = jnp.exp(-c)
        p_rest = jnp.exp(c_last - c)
        pre.append(dict(
            r=r, k=k, v=inp[i]["v"], rt=r * jnp.exp(c), at=-inp[i]["kk"] * jnp.exp(c - ld),
            kt=k * p_inv, bt=b * p_inv, kh=k * p_rest, bh=b * p_rest, p_all=jnp.exp(c_last)))
    get = lambda name: each(lambda i, g: pre[i][name][:, lanes(g)])

    s0 = each(lambda i, g: state[i, g])
    x = each(lambda i, g: jnp.concatenate(
        [pre[i]["at"][:, lanes(g)], pre[i]["rt"][:, lanes(g)]], axis=0).astype(BF16))
    abk = [_dot_nt(xi, jnp.concatenate([bd(bt_), bd(kt_), s_.astype(BF16)], axis=0))
           for xi, bt_, kt_, s_ in zip(x, get("bt"), get("kt"), s0)]
    ab = [m[:, :GROUP] for m in abk]
    ak = [m[:, GROUP:2 * GROUP] for m in abk]
    gs = [m[:, 2 * GROUP:] for m in abk]

    nm = masked("strict", [m[:CHUNK] for m in ab])
    eye = jnp.where(mk["eye"], 1.0, 0.0)
    levels = (("eye", "b2"), ("b2", "b4"), ("b4", "b8"), ("b8", "b16"), ("b16", "b32"),
              ("b32", None))
    q, xn = None, nm
    for lo, hi in levels:
        link = ~mk[lo] if hi is None else (mk[hi] & ~mk[lo])
        off = [jnp.where(link, x_, 0.0).astype(BF16) for x_ in xn]
        if q is None:
            q = [eye + o_ for o_ in off]
            xn = add(xn, pmm(off, xn))
        elif hi is None:
            q = add(q, pmm(off, q))
        else:
            both = [_dot(o_, jnp.concatenate([bd(q_), bd(x_)], axis=1))
                    for o_, q_, x_ in zip(off, q, xn)]
            q = [q_ + m[:, :GROUP] for q_, m in zip(q, both)]
            xn = [x_ + m[:, GROUP:] for x_, m in zip(xn, both)]

    a_rb = masked("incl", [m[CHUNK:] for m in ab])
    a_k = [jnp.concatenate([jnp.where(mk["strict"], m[:CHUNK], 0.0),
                            jnp.where(mk["incl"], m[CHUNK:], 0.0)], axis=0) for m in ak]
    vg = get("v")
    av = pmm(a_k, vg)
    w = [g_[:CHUNK] + a_[:CHUNK] for g_, a_ in zip(gs, av)]
    u = pmm(q, w)
    y = [g_[CHUNK:] + a_[CHUNK:] + c_ for g_, a_, c_ in zip(gs, av, pmm(a_rb, u))]
    ds = [_dot_tn(jnp.concatenate([v_, u_], axis=0).astype(BF16),
                  jnp.concatenate([kh_, bh_], axis=0).astype(BF16))
          for v_, u_, kh_, bh_ in zip(vg, u, get("kh"), get("bh"))]
    new_state = [jnp.where(bdmask, s_ * pre[i]["p_all"][:, lanes(g)] + ds_, 0.0)
                 for (i, g), s_, ds_ in zip(units, s0, ds)]

    rkr = each(lambda i, g: pre[i]["r"][:, lanes(g)] * pre[i]["k"][:, lanes(g)] * rk[:, lanes(g)])
    st1 = [_dot(jnp.concatenate(_split_bf16(y_, 2) + _split_bf16(q_, 2), axis=0), ones_bd)
           for y_, q_ in zip(y, rkr)]
    mean = [(s_[:CHUNK] + s_[CHUNK:2 * CHUNK]) * (1.0 / HEAD) for s_ in st1]
    yc = [y_ - m_ for y_, m_ in zip(y, mean)]
    st2 = [_dot(jnp.concatenate(_split_bf16(c_ * c_, 2), axis=0), ones_bd) for c_ in yc]
    outs = []
    for (i, g), yc_, s1, s2, v_ in zip(units, yc, st1, st2, vg):
        var = (s2[:CHUNK] + s2[CHUNK:]) * (1.0 / HEAD)
        yn = yc_ * lax.rsqrt(var + GN_EPS) * gg[:, lanes(g)] + gb[:, lanes(g)]
        bonus = (s1[2 * CHUNK:3 * CHUNK] + s1[3 * CHUNK:]) * v_
        z = inp[i]["z"][:, lanes(g)]
        outs.append(((yn + bonus) * (z * jax.nn.sigmoid(z))).astype(BF16))
    for (i, g), s_, o_ in zip(units, new_state, outs):
        s_ref[i, g] = s_
        o_ref[i, :, lanes(g)] = o_


def _row_spec(tm, width):
    return pl.BlockSpec((tm, width), lambda i: (i, 0))


def _full_spec(shape):
    return pl.BlockSpec(shape, lambda *_: (0,) * len(shape))


def _rope_tables(seq):
    half = HEAD // 2
    inv = ROPE_THETA ** (-jnp.arange(half, dtype=F32) / half)
    ang = jnp.arange(seq).astype(F32)[:, None] * inv[None, :]
    cos = jnp.cos(ang)
    sin = jnp.sin(ang)
    cos = jnp.tile(jnp.concatenate([cos, cos], axis=-1), (1, 128 // HEAD))
    sin = jnp.tile(jnp.concatenate([-sin, sin], axis=-1), (1, 128 // HEAD))
    return cos, sin


def _attn_in(x2, g, w, bias, seq, tm):
    n, d = x2.shape
    kv = N_KV * HEAD
    cos, sin = _rope_tables(seq)
    per_seq = seq // tm
    tab_spec = pl.BlockSpec((tm, 128), lambda i: (i % per_seq, 0))
    return pl.pallas_call(
        _attn_in_kernel,
        grid=(n // tm,),
        in_specs=[_row_spec(tm, d), _full_spec((1, d)), _full_spec(w.shape),
                  _full_spec((1, w.shape[1])), tab_spec, tab_spec],
        out_specs=[_row_spec(tm, d), _row_spec(tm, kv), _row_spec(tm, kv), _row_spec(tm, d)],
        out_shape=[jax.ShapeDtypeStruct((n, d), BF16), jax.ShapeDtypeStruct((n, kv), BF16),
                   jax.ShapeDtypeStruct((n, kv), BF16), jax.ShapeDtypeStruct((n, d), F32)],
        compiler_params=_cparams(1),
        name="attn_in",
    )(x2, g, w, bias, cos, sin)


def _swa(q, k, v, z, sinks, batch, seq):
    n, d = q.shape
    kv = k.shape[1]
    blocks = max(b_ for b_ in range(1, SWA_BLOCKS + 1) if (seq // WINDOW) % b_ == 0)
    nb = seq // (blocks * WINDOW)
    cur = lambda width: pl.BlockSpec((blocks * WINDOW, width), lambda b, j: (b * nb + j, 0))
    prev = lambda width: pl.BlockSpec(
        (WINDOW, width), lambda b, j: ((b * nb + j) * blocks - jnp.minimum(j, 1), 0))
    return pl.pallas_call(
        _swa_kernel,
        grid=(batch, nb),
        in_specs=[pl.BlockSpec(memory_space=pltpu.SMEM), cur(d), cur(kv), prev(kv), cur(kv),
                  prev(kv), cur(d)],
        out_specs=cur(d),
        out_shape=jax.ShapeDtypeStruct((n, d), BF16),
        compiler_params=_cparams(2),
        name="swa",
    )(sinks, q, k, k, v, v, z)


def _out_ple(og, h, p3, layer, wo, wg, wp, g, tm, emit_h):
    n, d = h.shape
    out_shape = [jax.ShapeDtypeStruct((n, d), F32)] * (2 if emit_h else 1)
    out_specs = [_row_spec(tm, d)] * (2 if emit_h else 1)
    return pl.pallas_call(
        functools.partial(_out_ple_kernel, emit_h=emit_h),
        grid=(n // tm,),
        in_specs=[_row_spec(tm, d), _row_spec(tm, d),
                  pl.BlockSpec((None, tm, p3.shape[2]), lambda i: (layer, i, 0)),
                  _full_spec(wo.shape), _full_spec(wg.shape), _full_spec(wp.shape),
                  _full_spec((1, d))],
        out_specs=out_specs,
        out_shape=out_shape,
        compiler_params=_cparams(1),
        name="out_ple_mid" if emit_h else "out_ple_last",
    )(og, h, p3, wo, wg, wp, g)


def _rwkv_in(hn, mu, w, w0, w1, w2, a0, a1, a2, k_k, k_a, seq, tm):
    n, d = hn.shape
    prev_spec = pl.BlockSpec((8, d), lambda i: (jnp.maximum(i * (tm // 8) - 1, 0), 0))
    vec = _full_spec((1, d))
    outs = 7
    return pl.pallas_call(
        functools.partial(_rwkv_in_kernel, tiles_per_seq=seq // tm),
        grid=(n // tm,),
        in_specs=[_row_spec(tm, d), prev_spec, _full_spec(mu.shape), _full_spec(w.shape), vec,
                  _full_spec(w1.shape), _full_spec(w2.shape), vec, _full_spec(a1.shape),
                  _full_spec(a2.shape), vec, vec],
        out_specs=[_row_spec(tm, d)] * outs,
        out_shape=[jax.ShapeDtypeStruct((n, d), F32)] * outs,
        compiler_params=_cparams(1),
        name="rwkv_in",
    )(hn, hn, mu, w, w0, w1, w2, a0, a1, a2, k_k, k_a)


def _rwkv_scan(r, k, v, kk, b, ld, z, r_k, gn_g, gn_b, batch, seq, rows):
    n, d = r.shape
    nc = seq // CHUNK
    blk = pl.BlockSpec((rows, CHUNK, d), lambda bi, ci: (bi, ci, 0))
    vec = pl.BlockSpec((1, d), lambda bi, ci: (0, 0))
    seqs = [a.reshape(batch, seq, d) for a in (r, k, v, kk, b, ld, z)]
    out = pl.pallas_call(
        _rwkv_scan_kernel,
        grid=(batch // rows, nc),
        in_specs=[blk] * 7 + [vec] * 3,
        out_specs=blk,
        out_shape=jax.ShapeDtypeStruct((batch, seq, d), BF16),
        scratch_shapes=[pltpu.VMEM((rows, d // GROUP, GROUP, GROUP), F32)],
        compiler_params=_cparams(2),
        name="rwkv_scan",
    )(*seqs, r_k, gn_g, gn_b)
    return out.reshape(n, d)


def kernel(x, p, norm_g, attn_w_in, attn_b_in, attn_sinks, attn_w_out, rwkv_mu, rwkv_w_in,
           rwkv_w0, rwkv_w1, rwkv_w2, rwkv_a0, rwkv_a1, rwkv_a2, rwkv_k_k, rwkv_k_a, rwkv_r_k,
           rwkv_gn_g, rwkv_gn_b, rwkv_w_out, ple_w_proj, ple_w_gate, final_norm_g):
    batch, seq, d = x.shape
    depth = p.shape[0]
    assert depth == 2 and d % GROUP == 0 and seq % WINDOW == 0
    n = batch * seq
    tm = min(1024, seq)
    tm_r = min(512, seq)
    row = lambda a: a.reshape(1, -1).astype(F32)
    bf = lambda a: a.astype(BF16)

    h = x.reshape(n, d)
    p3 = p.reshape(depth, n, p.shape[-1])

    q, k, v, z = _attn_in(h, row(norm_g[0]), bf(attn_w_in[0]), row(attn_b_in[0]), seq, tm)
    og = _swa(q, k, v, z, attn_sinks[0].astype(F32), batch, seq)
    h, hn = _out_ple(og, h, p3, 0,bf(attn_w_out[0]), bf(ple_w_gate[0]), bf(ple_w_proj[0]),
                     row(norm_g[1]), tm, emit_h=True)

    r, k, v, z, kk, b, ld = _rwkv_in(
        hn, rwkv_mu[0].astype(F32), bf(rwkv_w_in[0]), row(rwkv_w0[0]), bf(rwkv_w1[0]),
        bf(rwkv_w2[0]), row(rwkv_a0[0]), bf(rwkv_a1[0]), bf(rwkv_a2[0]), row(rwkv_k_k[0]),
        row(rwkv_k_a[0]), seq, tm_r)
    yg = _rwkv_scan(r, k, v, kk, b, ld, z, row(rwkv_r_k[0]), row(rwkv_gn_g[0]),
                    row(rwkv_gn_b[0]), batch, seq,
                    rows=max(r_ for r_ in range(1, SCAN_ROWS + 1) if batch % r_ == 0))
    (out,) = _out_ple(yg, h, p3, 1,bf(rwkv_w_out[0]), bf(ple_w_gate[1]), bf(ple_w_proj[1]),
                      row(final_norm_g), tm, emit_h=False)
    return out.reshape(batch, seq, d)
```

```python
import functools
import math

import jax
import jax.numpy as jnp
from jax import lax
from jax.experimental import pallas as pl
from jax.experimental.pallas import tpu as pltpu

F32 = jnp.float32
BF16 = jnp.bfloat16

HEAD = 64
N_KV = 4
GQA = 4
WINDOW = 128
ROPE_THETA = 10000.0
MASK_VALUE = -1e30
NORM_EPS = 1e-6
GN_EPS = 64e-5
DECAY_SCALE = math.exp(-0.5)
CHUNK = 64
GROUP = 128
SCAN_ROWS = 8
SWA_BLOCKS = 4
VMEM_LIMIT = 56 * 1024 * 1024


def _cparams(n_axes):
    return pltpu.CompilerParams(
        dimension_semantics=("arbitrary",) * n_axes, vmem_limit_bytes=VMEM_LIMIT)


def _rms(x, g):
    ms = jnp.mean(x * x, axis=-1, keepdims=True)
    return x * lax.rsqrt(ms + NORM_EPS) * g


def _split_bf16(x, terms):
    parts = []
    rem = x
    for _ in range(terms):
        p = rem.astype(BF16)
        parts.append(p)
        rem = rem - p.astype(F32)
    return parts


def _dot(a, b):
    return jnp.dot(a, b, preferred_element_type=F32)


def _dot_nt(a, b):
    return lax.dot_general(a, b, (((1,), (1,)), ((), ())), preferred_element_type=F32)


def _dot_tn(a, b):
    return lax.dot_general(a, b, (((0,), (0,)), ((), ())), preferred_element_type=F32)


def _seg_sum(x, ones_bd):
    hi, lo = _split_bf16(x, 2)
    return _dot(hi, ones_bd) + _dot(lo, ones_bd)


def _block_ones():
    r = lax.broadcasted_iota(jnp.int32, (GROUP, GROUP), 0) >> 6
    c = lax.broadcasted_iota(jnp.int32, (GROUP, GROUP), 1) >> 6
    return r == c


def _attn_in_kernel(x_ref, g_ref, w_ref, b_ref, cos_ref, sin_ref, q_ref, k_ref, v_ref, z_ref):
    hn = _rms(x_ref[...], g_ref[...]).astype(BF16)
    cos = cos_ref[...]
    sin = sin_ref[...]
    first = (lax.broadcasted_iota(jnp.int32, cos.shape, 1) & (HEAD - 1)) < HEAD // 2

    def rope(t):
        rot = jnp.where(first, pltpu.roll(t, 128 - HEAD // 2, 1), pltpu.roll(t, HEAD // 2, 1))
        return t * cos + rot * sin

    d = q_ref.shape[1]
    kv = k_ref.shape[1]
    q = _dot(hn, w_ref[:, 0:d]) + b_ref[:, 0:d]
    for c in range(d // 128):
        sl = slice(c * 128, (c + 1) * 128)
        q_ref[:, sl] = (rope(q[:, sl]) * (HEAD ** -0.5)).astype(BF16)
    k = _dot(hn, w_ref[:, d:d + kv]) + b_ref[:, d:d + kv]
    for c in range(kv // 128):
        sl = slice(c * 128, (c + 1) * 128)
        k_ref[:, sl] = rope(k[:, sl]).astype(BF16)
    v = _dot(hn, w_ref[:, d + kv:d + 2 * kv]) + b_ref[:, d + kv:d + 2 * kv]
    v_ref[...] = v.astype(BF16)
    z = _dot(hn, w_ref[:, d + 2 * kv:]) + b_ref[:, d + 2 * kv:]
    z_ref[...] = z * jax.nn.sigmoid(z)


def _swa_kernel(sink_ref, q_ref, kc_ref, kp_ref, vc_ref, vp_ref, zg_ref, o_ref):
    n = pl.program_id(1)
    blocks = q_ref.shape[0] // WINDOW
    ki = lax.broadcasted_iota(jnp.int32, (WINDOW, WINDOW), 0)
    qi = lax.broadcasted_iota(jnp.int32, (WINDOW, WINDOW), 1)
    from_prev = qi < ki
    kv_slices = [slice(kh * HEAD, (kh + 1) * HEAD) for kh in range(N_KV)]
    units = [(j, h) for j in range(blocks) for h in range(N_KV * GQA)]

    def window(cur_ref, prev_ref, j, ks):
        if j == 0:
            return jnp.concatenate([prev_ref[:, ks], cur_ref[0:WINDOW, ks]], axis=0)
        return cur_ref[(j - 1) * WINDOW:(j + 1) * WINDOW, ks]

    kcat = [[window(kc_ref, kp_ref, j, ks) for ks in kv_slices] for j in range(blocks)]
    vcat_t = [[window(vc_ref, vp_ref, j, ks).T for ks in kv_slices] for j in range(blocks)]
    s = [_dot_nt(kcat[j][h // GQA], q_ref[j * WINDOW:(j + 1) * WINDOW, h * HEAD:(h + 1) * HEAD])
         for j, h in units]

    def fold(j, x):
        prev = x[:WINDOW]
        if j == 0:
            prev = jnp.where(n > 0, prev, MASK_VALUE)
        return jnp.where(from_prev, prev, x[WINDOW:])

    s = [fold(j, x) for (j, h), x in zip(units, s)]
    m = [jnp.maximum(jnp.max(x, axis=0, keepdims=True), sink_ref[h]) for (j, h), x in zip(units, s)]
    p = [jnp.exp(x - m_) for x, m_ in zip(s, m)]
    l = [jnp.sum(x, axis=0, keepdims=True) + jnp.exp(sink_ref[h] - m_)
         for (j, h), x, m_ in zip(units, p, m)]
    pcat = [jnp.concatenate([jnp.where(from_prev, x, 0.0).astype(BF16),
                             jnp.where(from_prev, 0.0, x).astype(BF16)], axis=0) for x in p]
    o_t = [_dot(vcat_t[j][h // GQA], x) / l_ for (j, h), x, l_ in zip(units, pcat, l)]
    nh = N_KV * GQA
    for j in range(blocks):
        rows = slice(j * WINDOW, (j + 1) * WINDOW)
        o = jnp.concatenate(o_t[j * nh:(j + 1) * nh], axis=0).T
        o_ref[rows, :] = (o * zg_ref[rows, :]).astype(BF16)


def _out_ple_kernel(og_ref, h_ref, p_ref, wo_ref, wg_ref, wp_ref, g_ref, *out_refs, emit_h):
    h1 = h_ref[...] + _dot(og_ref[...], wo_ref[...])
    gate = jax.nn.sigmoid(_dot(h1.astype(BF16), wg_ref[...]))
    h2 = h1 + gate * _dot(p_ref[...].astype(BF16), wp_ref[...])
    if emit_h:
        out_refs[0][...] = h2
    out_refs[-1][...] = _rms(h2, g_ref[...])


def _rwkv_in_kernel(hn_ref, prev_ref, mu_ref, w_ref, w0_ref, w1_ref, w2_ref, a0_ref, a1_ref,
                    a2_ref, kk_ref, ka_ref, r_out, k_out, v_out, z_out, kk_out, b_out, ld_out,
                    *, tiles_per_seq):
    i = pl.program_id(0)
    hn = hn_ref[...]
    tm, d = hn.shape
    prev_row = jnp.where(i % tiles_per_seq == 0, 0.0, prev_ref[7:8, :])
    row = lax.broadcasted_iota(jnp.int32, (tm, d), 0)
    shifted = jnp.where(row == 0, prev_row, pltpu.roll(hn, 1, 0))
    xx = shifted - hn

    def lerp(c):
        return (hn + xx * mu_ref[c:c + 1, :]).astype(BF16)

    r_out[...] = _dot(lerp(0), w_ref[:, 0:d])
    k = _dot(lerp(1), w_ref[:, d:2 * d])
    v_out[...] = _dot(lerp(2), w_ref[:, 2 * d:3 * d])
    z_out[...] = _dot(lerp(3), w_ref[:, 3 * d:4 * d])

    lw = jnp.tanh(_dot(lerp(4), w1_ref[...])).astype(BF16)
    ld_out[...] = -DECAY_SCALE * jax.nn.sigmoid(w0_ref[...] + _dot(lw, w2_ref[...]))

    la = _dot(lerp(5), a1_ref[...]).astype(BF16)
    a = jax.nn.sigmoid(a0_ref[...] + _dot(la, a2_ref[...]))

    ones_bd = jnp.where(_block_ones(), 1.0, 0.0).astype(BF16)
    kk = k * kk_ref[...]
    for g in range(d // GROUP):
        sl = slice(g * GROUP, (g + 1) * GROUP)
        kg = kk[:, sl]
        nrm = jnp.sqrt(_seg_sum(kg * kg, ones_bd))
        kg = kg / jnp.maximum(nrm, 1e-12)
        kk_out[:, sl] = kg
        b_out[:, sl] = kg * a[:, sl]
    k_out[...] = k * (1.0 + (a - 1.0) * ka_ref[...])


def _scan_masks():
    t = lax.broadcasted_iota(jnp.int32, (CHUNK, GROUP), 0)
    s = lax.broadcasted_iota(jnp.int32, (CHUNK, GROUP), 1) & (HEAD - 1)
    same = lambda sh: (t >> sh) == (s >> sh)
    return dict(strict=s < t, incl=s <= t, eye=s == t, b2=same(1), b4=same(2), b8=same(3),
                b16=same(4), b32=same(5))


def _rwkv_scan_kernel(r_ref, k_ref, v_ref, kk_ref, b_ref, ld_ref, z_ref, rk_ref, gg_ref, gb_ref,
                      o_ref, s_ref):
    @pl.when(pl.program_id(1) == 0)
    def _():
        s_ref[...] = jnp.zeros_like(s_ref)

    rows, _, d = r_ref.shape
    lanes = lambda g: slice(g * GROUP, (g + 1) * GROUP)
    bdmask = _block_ones()
    ones_bd = jnp.where(bdmask, 1.0, 0.0).astype(BF16)
    mk = _scan_masks()

    def bd(x):
        return jnp.where(bdmask, jnp.tile(x.astype(BF16), (GROUP // HEAD, 1)), 0)

    def pmm(a_list, b_list):
        return [_dot(a.astype(BF16), bd(b)) for a, b in zip(a_list, b_list)]

    def add(a_list, b_list):
        return [a + b for a, b in zip(a_list, b_list)]

    def masked(name, x_list):
        return [jnp.where(mk[name], x, 0.0) for x in x_list]

    ti = lax.broadcasted_iota(jnp.int32, (CHUNK, 3 * CHUNK), 0)
    tj = lax.broadcasted_iota(jnp.int32, (CHUNK, 3 * CHUNK), 1) & (CHUNK - 1)
    tri3 = jnp.where(tj <= ti, 1.0, 0.0).astype(BF16)

    names = ("r", "k", "v", "kk", "b", "ld", "z")
    refs = (r_ref, k_ref, v_ref, kk_ref, b_ref, ld_ref, z_ref)
    inp = [{nm_: ref[i] for nm_, ref in zip(names, refs)} for i in range(rows)]
    state = {(i, g): s_ref[i, g] for i in range(rows) for g in range(d // GROUP)}
    rk, gg, gb = rk_ref[...], gg_ref[...], gb_ref[...]

    units = [(i, g) for i in range(rows) for g in range(d // GROUP)]
    each = lambda f: [f(i, g) for i, g in units]
    pre = []
    for i in range(rows):
        ld = inp[i]["ld"]
        c = _dot(tri3, jnp.concatenate(_split_bf16(ld, 3), axis=0))
        c_last = c[CHUNK - 1:CHUNK, :]
        r, k, b = inp[i]["r"], inp[i]["k"], inp[i]["b"]
        p_inv = jnp.exp(-c)
        p_rest = jnp.exp(c_last - c)
        pre.append(dict(
            r=r, k=k, v=inp[i]["v"], rt=r * jnp.exp(c), at=-inp[i]["kk"] * jnp.exp(c - ld),
            kt=k * p_inv, bt=b * p_inv, kh=k * p_rest, bh=b * p_rest, p_all=jnp.exp(c_last)))
    get = lambda name: each(lambda i, g: pre[i][name][:, lanes(g)])

    s0 = each(lambda i, g: state[i, g])
    x = each(lambda i, g: jnp.concatenate(
        [pre[i]["at"][:, lanes(g)], pre[i]["rt"][:, lanes(g)]], axis=0).astype(BF16))
    abk = [_dot_nt(xi, jnp.concatenate([bd(bt_), bd(kt_), s_.astype(BF16)], axis=0))
           for xi, bt_, kt_, s_ in zip(x, get("bt"), get("kt"), s0)]
    ab = [m[:, :GROUP] for m in abk]
    ak = [m[:, GROUP:2 * GROUP] for m in abk]
    gs = [m[:, 2 * GROUP:] for m in abk]

    nm = masked("strict", [m[:CHUNK] for m in ab])
    eye = jnp.where(mk["eye"], 1.0, 0.0)
    levels = (("eye", "b2"), ("b2", "b4"), ("b4", "b8"), ("b8", "b16"), ("b16", "b32"),
              ("b32", None))
    q, xn = None, nm
    for lo, hi in levels:
        link = ~mk[lo] if hi is None else (mk[hi] & ~mk[lo])
        off = [jnp.where(link, x_, 0.0).astype(BF16) for x_ in xn]
        if q is None:
            q = [eye + o_ for o_ in off]
            xn = add(xn, pmm(off, xn))
        elif hi is None:
            q = add(q, pmm(off, q))
        else:
            both = [_dot(o_, jnp.concatenate([bd(q_), bd(x_)], axis=1))
                    for o_, q_, x_ in zip(off, q, xn)]
            q = [q_ + m[:, :GROUP] for q_, m in zip(q, both)]
            xn = [x_ + m[:, GROUP:] for x_, m in zip(xn, both)]

    a_rb = masked("incl", [m[CHUNK:] for m in ab])
    a_k = [jnp.concatenate([jnp.where(mk["strict"], m[:CHUNK], 0.0),
                            jnp.where(mk["incl"], m[CHUNK:], 0.0)], axis=0) for m in ak]
    vg = get("v")
    av = pmm(a_k, vg)
    w = [g_[:CHUNK] + a_[:CHUNK] for g_, a_ in zip(gs, av)]
    u = pmm(q, w)
    y = [g_[CHUNK:] + a_[CHUNK:] + c_ for g_, a_, c_ in zip(gs, av, pmm(a_rb, u))]
    ds = [_dot_tn(jnp.concatenate([v_, u_], axis=0).astype(BF16),
                  jnp.concatenate([kh_, bh_], axis=0).astype(BF16))
          for v_, u_, kh_, bh_ in zip(vg, u, get("kh"), get("bh"))]
    new_state = [jnp.where(bdmask, s_ * pre[i]["p_all"][:, lanes(g)] + ds_, 0.0)
                 for (i, g), s_, ds_ in zip(units, s0, ds)]

    rkr = each(lambda i, g: pre[i]["r"][:, lanes(g)] * pre[i]["k"][:, lanes(g)] * rk[:, lanes(g)])
    st1 = [_dot(jnp.concatenate(_split_bf16(y_, 2) + _split_bf16(q_, 2), axis=0), ones_bd)
           for y_, q_ in zip(y, rkr)]
    mean = [(s_[:CHUNK] + s_[CHUNK:2 * CHUNK]) * (1.0 / HEAD) for s_ in st1]
    yc = [y_ - m_ for y_, m_ in zip(y, mean)]
    st2 = [_dot(jnp.concatenate(_split_bf16(c_ * c_, 2), axis=0), ones_bd) for c_ in yc]
    outs = []
    for (i, g), yc_, s1, s2, v_ in zip(units, yc, st1, st2, vg):
        var = (s2[:CHUNK] + s2[CHUNK:]) * (1.0 / HEAD)
        yn = yc_ * lax.rsqrt(var + GN_EPS) * gg[:, lanes(g)] + gb[:, lanes(g)]
        bonus = (s1[2 * CHUNK:3 * CHUNK] + s1[3 * CHUNK:]) * v_
        z = inp[i]["z"][:, lanes(g)]
        outs.append(((yn + bonus) * (z * jax.nn.sigmoid(z))).astype(BF16))
    for (i, g), s_, o_ in zip(units, new_state, outs):
        s_ref[i, g] = s_
        o_ref[i, :, lanes(g)] = o_


def _row_spec(tm, width):
    return pl.BlockSpec((tm, width), lambda i: (i, 0))


def _full_spec(shape):
    return pl.BlockSpec(shape, lambda *_: (0,) * len(shape))


def _rope_tables(seq):
    half = HEAD // 2
    inv = ROPE_THETA ** (-jnp.arange(half, dtype=F32) / half)
    ang = jnp.arange(seq).astype(F32)[:, None] * inv[None, :]
    cos = jnp.cos(ang)
    sin = jnp.sin(ang)
    cos = jnp.tile(jnp.concatenate([cos, cos], axis=-1), (1, 128 // HEAD))
    sin = jnp.tile(jnp.concatenate([-sin, sin], axis=-1), (1, 128 // HEAD))
    return cos, sin


def _attn_in(x2, g, w, bias, seq, tm):
    n, d = x2.shape
    kv = N_KV * HEAD
    cos, sin = _rope_tables(seq)
    per_seq = seq // tm
    tab_spec = pl.BlockSpec((tm, 128), lambda i: (i % per_seq, 0))
    return pl.pallas_call(
        _attn_in_kernel,
        grid=(n // tm,),
        in_specs=[_row_spec(tm, d), _full_spec((1, d)), _full_spec(w.shape),
                  _full_spec((1, w.shape[1])), tab_spec, tab_spec],
        out_specs=[_row_spec(tm, d), _row_spec(tm, kv), _row_spec(tm, kv), _row_spec(tm, d)],
        out_shape=[jax.ShapeDtypeStruct((n, d), BF16), jax.ShapeDtypeStruct((n, kv), BF16),
                   jax.ShapeDtypeStruct((n, kv), BF16), jax.ShapeDtypeStruct((n, d), F32)],
        compiler_params=_cparams(1),
        name="attn_in",
    )(x2, g, w, bias, cos, sin)


def _swa(q, k, v, z, sinks, batch, seq):
    n, d = q.shape
    kv = k.shape[1]
    blocks = max(b_ for b_ in range(1, SWA_BLOCKS + 1) if (seq // WINDOW) % b_ == 0)
    nb = seq // (blocks * WINDOW)
    cur = lambda width: pl.BlockSpec((blocks * WINDOW, width), lambda b, j: (b * nb + j, 0))
    prev = lambda width: pl.BlockSpec(
        (WINDOW, width), lambda b, j: ((b * nb + j) * blocks - jnp.minimum(j, 1), 0))
    return pl.pallas_call(
        _swa_kernel,
        grid=(batch, nb),
        in_specs=[pl.BlockSpec(memory_space=pltpu.SMEM), cur(d), cur(kv), prev(kv), cur(kv),
                  prev(kv), cur(d)],
        out_specs=cur(d),
        out_shape=jax.ShapeDtypeStruct((n, d), BF16),
        compiler_params=_cparams(2),
        name="swa",
    )(sinks, q, k, k, v, v, z)


def _out_ple(og, h, p3, layer, wo, wg, wp, g, tm, emit_h):
    n, d = h.shape
    out_shape = [jax.ShapeDtypeStruct((n, d), F32)] * (2 if emit_h else 1)
    out_specs = [_row_spec(tm, d)] * (2 if emit_h else 1)
    return pl.pallas_call(
        functools.partial(_out_ple_kernel, emit_h=emit_h),
        grid=(n // tm,),
        in_specs=[_row_spec(tm, d), _row_spec(tm, d),
                  pl.BlockSpec((None, tm, p3.shape[2]), lambda i: (layer, i, 0)),
                  _full_spec(wo.shape), _full_spec(wg.shape), _full_spec(wp.shape),
                  _full_spec((1, d))],
        out_specs=out_specs,
        out_shape=out_shape,
        compiler_params=_cparams(1),
        name="out_ple_mid" if emit_h else "out_ple_last",
    )(og, h, p3, wo, wg, wp, g)


def _rwkv_in(hn, mu, w, w0, w1, w2, a0, a1, a2, k_k, k_a, seq, tm):
    n, d = hn.shape
    prev_spec = pl.BlockSpec((8, d), lambda i: (jnp.maximum(i * (tm // 8) - 1, 0), 0))
    vec = _full_spec((1, d))
    outs = 7
    return pl.pallas_call(
        functools.partial(_rwkv_in_kernel, tiles_per_seq=seq // tm),
        grid=(n // tm,),
        in_specs=[_row_spec(tm, d), prev_spec, _full_spec(mu.shape), _full_spec(w.shape), vec,
                  _full_spec(w1.shape), _full_spec(w2.shape), vec, _full_spec(a1.shape),
                  _full_spec(a2.shape), vec, vec],
        out_specs=[_row_spec(tm, d)] * outs,
        out_shape=[jax.ShapeDtypeStruct((n, d), F32)] * outs,
        compiler_params=_cparams(1),
        name="rwkv_in",
    )(hn, hn, mu, w, w0, w1, w2, a0, a1, a2, k_k, k_a)


def _rwkv_scan(r, k, v, kk, b, ld, z, r_k, gn_g, gn_b, batch, seq, rows):
    n, d = r.shape
    nc = seq // CHUNK
    blk = pl.BlockSpec((rows, CHUNK, d), lambda bi, ci: (bi, ci, 0))
    vec = pl.BlockSpec((1, d), lambda bi, ci: (0, 0))
    seqs = [a.reshape(batch, seq, d) for a in (r, k, v, kk, b, ld, z)]
    out = pl.pallas_call(
        _rwkv_scan_kernel,
        grid=(batch // rows, nc),
        in_specs=[blk] * 7 + [vec] * 3,
        out_specs=blk,
        out_shape=jax.ShapeDtypeStruct((batch, seq, d), BF16),
        scratch_shapes=[pltpu.VMEM((rows, d // GROUP, GROUP, GROUP), F32)],
        compiler_params=_cparams(2),
        name="rwkv_scan",
    )(*seqs, r_k, gn_g, gn_b)
    return out.reshape(n, d)


def kernel(x, p, norm_g, attn_w_in, attn_b_in, attn_sinks, attn_w_out, rwkv_mu, rwkv_w_in,
           rwkv_w0, rwkv_w1, rwkv_w2, rwkv_a0, rwkv_a1, rwkv_a2, rwkv_k_k, rwkv_k_a, rwkv_r_k,
           rwkv_gn_g, rwkv_gn_b, rwkv_w_out, ple_w_proj, ple_w_gate, final_norm_g):
    batch, seq, d = x.shape
    depth = p.shape[0]
    assert depth == 2 and d % GROUP == 0 and seq % WINDOW == 0
    n = batch * seq
    tm = min(1024, seq)
    tm_r = min(512, seq)
    row = lambda a: a.reshape(1, -1).astype(F32)
    bf = lambda a: a.astype(BF16)

    h = x.reshape(n, d)
    p3 = p.reshape(depth, n, p.shape[-1])

    q, k, v, z = _attn_in(h, row(norm_g[0]), bf(attn_w_in[0]), row(attn_b_in[0]), seq, tm)
    og = _swa(q, k, v, z, attn_sinks[0].astype(F32), batch, seq)
    h, hn = _out_ple(og, h, p3, 0,bf(attn_w_out[0]), bf(ple_w_gate[0]), bf(ple_w_proj[0]),
                     row(norm_g[1]), tm, emit_h=True)

    r, k, v, z, kk, b, ld = _rwkv_in(
        hn, rwkv_mu[0].astype(F32), bf(rwkv_w_in[0]), row(rwkv_w0[0]), bf(rwkv_w1[0]),
        bf(rwkv_w2[0]), row(rwkv_a0[0]), bf(rwkv_a1[0]), bf(rwkv_a2[0]), row(rwkv_k_k[0]),
        row(rwkv_k_a[0]), seq, tm_r)
    yg = _rwkv_scan(r, k, v, kk, b, ld, z, row(rwkv_r_k[0]), row(rwkv_gn_g[0]),
                    row(rwkv_gn_b[0]), batch, seq,
                    rows=max(r_ for r_ in range(1, SCAN_ROWS + 1) if batch % r_ == 0))
    (out,) = _out_ple(yg, h, p3, 1,bf(rwkv_w_out[0]), bf(ple_w_gate[1]), bf(ple_w_proj[1]),
                      row(final_norm_g), tm, emit_h=False)
    return out.reshape(batch, seq, d)
```

```python
import functools
import math

import jax
import jax.numpy as jnp
from jax import lax
from jax.experimental import pallas as pl
from jax.experimental.pallas import tpu as pltpu

F32 = jnp.float32
BF16 = jnp.bfloat16

HEAD = 64
N_KV = 4
GQA = 4
WINDOW = 128
ROPE_THETA = 10000.0
MASK_VALUE = -1e30
NORM_EPS = 1e-6
GN_EPS = 64e-5
DECAY_SCALE = math.exp(-0.5)
CHUNK = 64
GROUP = 128
SCAN_ROWS = 8
SWA_BLOCKS = 8
VMEM_LIMIT = 56 * 1024 * 1024


def _cparams(n_axes):
    return pltpu.CompilerParams(
        dimension_semantics=("arbitrary",) * n_axes, vmem_limit_bytes=VMEM_LIMIT)


def _rms(x, g):
    ms = jnp.mean(x * x, axis=-1, keepdims=True)
    return x * lax.rsqrt(ms + NORM_EPS) * g


def _split_bf16(x, terms):
    parts = []
    rem = x
    for _ in range(terms):
        p = rem.astype(BF16)
        parts.append(p)
        rem = rem - p.astype(F32)
    return parts


def _dot(a, b):
    return jnp.dot(a, b, preferred_element_type=F32)


def _dot_nt(a, b):
    return lax.dot_general(a, b, (((1,), (1,)), ((), ())), preferred_element_type=F32)


def _dot_tn(a, b):
    return lax.dot_general(a, b, (((0,), (0,)), ((), ())), preferred_element_type=F32)


def _seg_sum(x, ones_bd):
    hi, lo = _split_bf16(x, 2)
    return _dot(hi, ones_bd) + _dot(lo, ones_bd)


def _block_ones():
    r = lax.broadcasted_iota(jnp.int32, (GROUP, GROUP), 0) >> 6
    c = lax.broadcasted_iota(jnp.int32, (GROUP, GROUP), 1) >> 6
    return r == c


def _attn_in_kernel(x_ref, g_ref, w_ref, b_ref, cos_ref, sin_ref, q_ref, k_ref, v_ref, z_ref):
    hn = _rms(x_ref[...], g_ref[...]).astype(BF16)
    cos = cos_ref[...]
    sin = sin_ref[...]
    first = (lax.broadcasted_iota(jnp.int32, cos.shape, 1) & (HEAD - 1)) < HEAD // 2

    def rope(t):
        rot = jnp.where(first, pltpu.roll(t, 128 - HEAD // 2, 1), pltpu.roll(t, HEAD // 2, 1))
        return t * cos + rot * sin

    d = q_ref.shape[1]
    kv = k_ref.shape[1]
    q = _dot(hn, w_ref[:, 0:d]) + b_ref[:, 0:d]
    for c in range(d // 128):
        sl = slice(c * 128, (c + 1) * 128)
        q_ref[:, sl] = (rope(q[:, sl]) * (HEAD ** -0.5)).astype(BF16)
    k = _dot(hn, w_ref[:, d:d + kv]) + b_ref[:, d:d + kv]
    for c in range(kv // 128):
        sl = slice(c * 128, (c + 1) * 128)
        k_ref[:, sl] = rope(k[:, sl]).astype(BF16)
    v = _dot(hn, w_ref[:, d + kv:d + 2 * kv]) + b_ref[:, d + kv:d + 2 * kv]
    v_ref[...] = v.astype(BF16)
    z = _dot(hn, w_ref[:, d + 2 * kv:]) + b_ref[:, d + 2 * kv:]
    z_ref[...] = z * jax.nn.sigmoid(z)


def _swa_kernel(sink_ref, q_ref, kc_ref, kp_ref, vc_ref, vp_ref, zg_ref, o_ref):
    n = pl.program_id(1)
    blocks = q_ref.shape[0] // WINDOW
    ki = lax.broadcasted_iota(jnp.int32, (WINDOW, WINDOW), 0)
    qi = lax.broadcasted_iota(jnp.int32, (WINDOW, WINDOW), 1)
    from_prev = qi < ki
    kv_slices = [slice(kh * HEAD, (kh + 1) * HEAD) for kh in range(N_KV)]
    units = [(j, h) for j in range(blocks) for h in range(N_KV * GQA)]

    def window(cur_ref, prev_ref, j, ks):
        if j == 0:
            return jnp.concatenate([prev_ref[:, ks], cur_ref[0:WINDOW, ks]], axis=0)
        return cur_ref[(j - 1) * WINDOW:(j + 1) * WINDOW, ks]

    kcat = [[window(kc_ref, kp_ref, j, ks) for ks in kv_slices] for j in range(blocks)]
    vcat_t = [[window(vc_ref, vp_ref, j, ks).T for ks in kv_slices] for j in range(blocks)]
    s = [_dot_nt(kcat[j][h // GQA], q_ref[j * WINDOW:(j + 1) * WINDOW, h * HEAD:(h + 1) * HEAD])
         for j, h in units]

    def fold(j, x):
        prev = x[:WINDOW]
        if j == 0:
            prev = jnp.where(n > 0, prev, MASK_VALUE)
        return jnp.where(from_prev, prev, x[WINDOW:])

    s = [fold(j, x) for (j, h), x in zip(units, s)]
    m = [jnp.maximum(jnp.max(x, axis=0, keepdims=True), sink_ref[h]) for (j, h), x in zip(units, s)]
    p = [jnp.exp(x - m_) for x, m_ in zip(s, m)]
    l = [jnp.sum(x, axis=0, keepdims=True) + jnp.exp(sink_ref[h] - m_)
         for (j, h), x, m_ in zip(units, p, m)]
    pcat = [jnp.concatenate([jnp.where(from_prev, x, 0.0).astype(BF16),
                             jnp.where(from_prev, 0.0, x).astype(BF16)], axis=0) for x in p]
    o_t = [_dot(vcat_t[j][h // GQA], x) / l_ for (j, h), x, l_ in zip(units, pcat, l)]
    nh = N_KV * GQA
    for j in range(blocks):
        rows = slice(j * WINDOW, (j + 1) * WINDOW)
        o = jnp.concatenate(o_t[j * nh:(j + 1) * nh], axis=0).T
        o_ref[rows, :] = (o * zg_ref[rows, :]).astype(BF16)


def _out_ple_kernel(og_ref, h_ref, p_ref, wo_ref, wg_ref, wp_ref, g_ref, *out_refs, emit_h):
    h1 = h_ref[...] + _dot(og_ref[...], wo_ref[...])
    gate = jax.nn.sigmoid(_dot(h1.astype(BF16), wg_ref[...]))
    h2 = h1 + gate * _dot(p_ref[...].astype(BF16), wp_ref[...])
    if emit_h:
        out_refs[0][...] = h2
    out_refs[-1][...] = _rms(h2, g_ref[...])


def _rwkv_in_kernel(hn_ref, prev_ref, mu_ref, w_ref, w0_ref, w1_ref, w2_ref, a0_ref, a1_ref,
                    a2_ref, kk_ref, ka_ref, r_out, k_out, v_out, z_out, kk_out, b_out, ld_out,
                    *, tiles_per_seq):
    i = pl.program_id(0)
    hn = hn_ref[...]
    tm, d = hn.shape
    prev_row = jnp.where(i % tiles_per_seq == 0, 0.0, prev_ref[7:8, :])
    row = lax.broadcasted_iota(jnp.int32, (tm, d), 0)
    shifted = jnp.where(row == 0, prev_row, pltpu.roll(hn, 1, 0))
    xx = shifted - hn

    def lerp(c):
        return (hn + xx * mu_ref[c:c + 1, :]).astype(BF16)

    r_out[...] = _dot(lerp(0), w_ref[:, 0:d])
    k = _dot(lerp(1), w_ref[:, d:2 * d])
    v_out[...] = _dot(lerp(2), w_ref[:, 2 * d:3 * d])
    z_out[...] = _dot(lerp(3), w_ref[:, 3 * d:4 * d])

    lw = jnp.tanh(_dot(lerp(4), w1_ref[...])).astype(BF16)
    ld_out[...] = -DECAY_SCALE * jax.nn.sigmoid(w0_ref[...] + _dot(lw, w2_ref[...]))

    la = _dot(lerp(5), a1_ref[...]).astype(BF16)
    a = jax.nn.sigmoid(a0_ref[...] + _dot(la, a2_ref[...]))

    ones_bd = jnp.where(_block_ones(), 1.0, 0.0).astype(BF16)
    kk = k * kk_ref[...]
    for g in range(d // GROUP):
        sl = slice(g * GROUP, (g + 1) * GROUP)
        kg = kk[:, sl]
        nrm = jnp.sqrt(_seg_sum(kg * kg, ones_bd))
        kg = kg / jnp.maximum(nrm, 1e-12)
        kk_out[:, sl] = kg
        b_out[:, sl] = kg * a[:, sl]
    k_out[...] = k * (1.0 + (a - 1.0) * ka_ref[...])


def _scan_masks():
    t = lax.broadcasted_iota(jnp.int32, (CHUNK, GROUP), 0)
    s = lax.broadcasted_iota(jnp.int32, (CHUNK, GROUP), 1) & (HEAD - 1)
    same = lambda sh: (t >> sh) == (s >> sh)
    return dict(strict=s < t, incl=s <= t, eye=s == t, b2=same(1), b4=same(2), b8=same(3),
                b16=same(4), b32=same(5))


def _rwkv_scan_kernel(r_ref, k_ref, v_ref, kk_ref, b_ref, ld_ref, z_ref, rk_ref, gg_ref, gb_ref,
                      o_ref, s_ref):
    @pl.when(pl.program_id(1) == 0)
    def _():
        s_ref[...] = jnp.zeros_like(s_ref)

    rows, _, d = r_ref.shape
    lanes = lambda g: slice(g * GROUP, (g + 1) * GROUP)
    bdmask = _block_ones()
    ones_bd = jnp.where(bdmask, 1.0, 0.0).astype(BF16)
    mk = _scan_masks()

    def bd(x):
        return jnp.where(bdmask, jnp.tile(x.astype(BF16), (GROUP // HEAD, 1)), 0)

    def pmm(a_list, b_list):
        return [_dot(a.astype(BF16), bd(b)) for a, b in zip(a_list, b_list)]

    def add(a_list, b_list):
        return [a + b for a, b in zip(a_list, b_list)]

    def masked(name, x_list):
        return [jnp.where(mk[name], x, 0.0) for x in x_list]

    ti = lax.broadcasted_iota(jnp.int32, (CHUNK, 3 * CHUNK), 0)
    tj = lax.broadcasted_iota(jnp.int32, (CHUNK, 3 * CHUNK), 1) & (CHUNK - 1)
    tri3 = jnp.where(tj <= ti, 1.0, 0.0).astype(BF16)

    names = ("r", "k", "v", "kk", "b", "ld", "z")
    refs = (r_ref, k_ref, v_ref, kk_ref, b_ref, ld_ref, z_ref)
    inp = [{nm_: ref[i] for nm_, ref in zip(names, refs)} for i in range(rows)]
    state = {(i, g): s_ref[i, g] for i in range(rows) for g in range(d // GROUP)}
    rk, gg, gb = rk_ref[...], gg_ref[...], gb_ref[...]

    units = [(i, g) for i in range(rows) for g in range(d // GROUP)]
    each = lambda f: [f(i, g) for i, g in units]
    pre = []
    for i in range(rows):
        ld = inp[i]["ld"]
        c = _dot(tri3, jnp.concatenate(_split_bf16(ld, 3), axis=0))
        c_last = c[CHUNK - 1:CHUNK, :]
        r, k, b = inp[i]["r"], inp[i]["k"], inp[i]["b"]
        p_inv = jnp.exp(-c)
        p_rest = jnp.exp(c_last - c)
        pre.append(dict(
            r=r, k=k, v=inp[i]["v"], rt=r * jnp.exp(c), at=-inp[i]["kk"] * jnp.exp(c - ld),
            kt=k * p_inv, bt=b * p_inv, kh=k * p_rest, bh=b * p_rest, p_all=jnp.exp(c_last)))
    get = lambda name: each(lambda i, g: pre[i][name][:, lanes(g)])

    s0 = each(lambda i, g: state[i, g])
    x = each(lambda i, g: jnp.concatenate(
        [pre[i]["at"][:, lanes(g)], pre[i]["rt"][:, lanes(g)]], axis=0).astype(BF16))
    abk = [_dot_nt(xi, jnp.concatenate([bd(bt_), bd(kt_), s_.astype(BF16)], axis=0))
           for xi, bt_, kt_, s_ in zip(x, get("bt"), get("kt"), s0)]
    ab = [m[:, :GROUP] for m in abk]
    ak = [m[:, GROUP:2 * GROUP] for m in abk]
    gs = [m[:, 2 * GROUP:] for m in abk]

    nm = masked("strict", [m[:CHUNK] for m in ab])
    eye = jnp.where(mk["eye"], 1.0, 0.0)
    levels = (("eye", "b2"), ("b2", "b4"), ("b4", "b8"), ("b8", "b16"), ("b16", "b32"),
              ("b32", None))
    q, xn = None, nm
    for lo, hi in levels:
        link = ~mk[lo] if hi is None else (mk[hi] & ~mk[lo])
        off = [jnp.where(link, x_, 0.0).astype(BF16) for x_ in xn]
        if q is None:
            q = [eye + o_ for o_ in off]
            xn = add(xn, pmm(off, xn))
        elif hi is None:
            q = add(q, pmm(off, q))
        else:
            both = [_dot(o_, jnp.concatenate([bd(q_), bd(x_)], axis=1))
                    for o_, q_, x_ in zip(off, q, xn)]
            q = [q_ + m[:, :GROUP] for q_, m in zip(q, both)]
            xn = [x_ + m[:, GROUP:] for x_, m in zip(xn, both)]

    a_rb = masked("incl", [m[CHUNK:] for m in ab])
    a_k = [jnp.concatenate([jnp.where(mk["strict"], m[:CHUNK], 0.0),
                            jnp.where(mk["incl"], m[CHUNK:], 0.0)], axis=0) for m in ak]
    vg = get("v")
    av = pmm(a_k, vg)
    w = [g_[:CHUNK] + a_[:CHUNK] for g_, a_ in zip(gs, av)]
    u = pmm(q, w)
    y = [g_[CHUNK:] + a_[CHUNK:] + c_ for g_, a_, c_ in zip(gs, av, pmm(a_rb, u))]
    ds = [_dot_tn(jnp.concatenate([v_, u_], axis=0).astype(BF16),
                  jnp.concatenate([kh_, bh_], axis=0).astype(BF16))
          for v_, u_, kh_, bh_ in zip(vg, u, get("kh"), get("bh"))]
    new_state = [jnp.where(bdmask, s_ * pre[i]["p_all"][:, lanes(g)] + ds_, 0.0)
                 for (i, g), s_, ds_ in zip(units, s0, ds)]

    rkr = each(lambda i, g: pre[i]["r"][:, lanes(g)] * pre[i]["k"][:, lanes(g)] * rk[:, lanes(g)])
    st1 = [_dot(jnp.concatenate(_split_bf16(y_, 2) + _split_bf16(q_, 2), axis=0), ones_bd)
           for y_, q_ in zip(y, rkr)]
    mean = [(s_[:CHUNK] + s_[CHUNK:2 * CHUNK]) * (1.0 / HEAD) for s_ in st1]
    yc = [y_ - m_ for y_, m_ in zip(y, mean)]
    st2 = [_dot(jnp.concatenate(_split_bf16(c_ * c_, 2), axis=0), ones_bd) for c_ in yc]
    outs = []
    for (i, g), yc_, s1, s2, v_ in zip(units, yc, st1, st2, vg):
        var = (s2[:CHUNK] + s2[CHUNK:]) * (1.0 / HEAD)
        yn = yc_ * lax.rsqrt(var + GN_EPS) * gg[:, lanes(g)] + gb[:, lanes(g)]
        bonus = (s1[2 * CHUNK:3 * CHUNK] + s1[3 * CHUNK:]) * v_
        z = inp[i]["z"][:, lanes(g)]
        outs.append(((yn + bonus) * (z * jax.nn.sigmoid(z))).astype(BF16))
    for (i, g), s_, o_ in zip(units, new_state, outs):
        s_ref[i, g] = s_
        o_ref[i, :, lanes(g)] = o_


def _row_spec(tm, width):
    return pl.BlockSpec((tm, width), lambda i: (i, 0))


def _full_spec(shape):
    return pl.BlockSpec(shape, lambda *_: (0,) * len(shape))


def _rope_tables(seq):
    half = HEAD // 2
    inv = ROPE_THETA ** (-jnp.arange(half, dtype=F32) / half)
    ang = jnp.arange(seq).astype(F32)[:, None] * inv[None, :]
    cos = jnp.cos(ang)
    sin = jnp.sin(ang)
    cos = jnp.tile(jnp.concatenate([cos, cos], axis=-1), (1, 128 // HEAD))
    sin = jnp.tile(jnp.concatenate([-sin, sin], axis=-1), (1, 128 // HEAD))
    return cos, sin


def _attn_in(x2, g, w, bias, seq, tm):
    n, d = x2.shape
    kv = N_KV * HEAD
    cos, sin = _rope_tables(seq)
    per_seq = seq // tm
    tab_spec = pl.BlockSpec((tm, 128), lambda i: (i % per_seq, 0))
    return pl.pallas_call(
        _attn_in_kernel,
        grid=(n // tm,),
        in_specs=[_row_spec(tm, d), _full_spec((1, d)), _full_spec(w.shape),
                  _full_spec((1, w.shape[1])), tab_spec, tab_spec],
        out_specs=[_row_spec(tm, d), _row_spec(tm, kv), _row_spec(tm, kv), _row_spec(tm, d)],
        out_shape=[jax.ShapeDtypeStruct((n, d), BF16), jax.ShapeDtypeStruct((n, kv), BF16),
                   jax.ShapeDtypeStruct((n, kv), BF16), jax.ShapeDtypeStruct((n, d), F32)],
        compiler_params=_cparams(1),
        name="attn_in",
    )(x2, g, w, bias, cos, sin)


def _swa(q, k, v, z, sinks, batch, seq):
    n, d = q.shape
    kv = k.shape[1]
    blocks = max(b_ for b_ in range(1, SWA_BLOCKS + 1) if (seq // WINDOW) % b_ == 0)
    nb = seq // (blocks * WINDOW)
    cur = lambda width: pl.BlockSpec((blocks * WINDOW, width), lambda b, j: (b * nb + j, 0))
    prev = lambda width: pl.BlockSpec(
        (WINDOW, width), lambda b, j: ((b * nb + j) * blocks - jnp.minimum(j, 1), 0))
    return pl.pallas_call(
        _swa_kernel,
        grid=(batch, nb),
        in_specs=[pl.BlockSpec(memory_space=pltpu.SMEM), cur(d), cur(kv), prev(kv), cur(kv),
                  prev(kv), cur(d)],
        out_specs=cur(d),
        out_shape=jax.ShapeDtypeStruct((n, d), BF16),
        compiler_params=_cparams(2),
        name="swa",
    )(sinks, q, k, k, v, v, z)


def _out_ple(og, h, p3, layer, wo, wg, wp, g, tm, emit_h):
    n, d = h.shape
    out_shape = [jax.ShapeDtypeStruct((n, d), F32)] * (2 if emit_h else 1)
    out_specs = [_row_spec(tm, d)] * (2 if emit_h else 1)
    return pl.pallas_call(
        functools.partial(_out_ple_kernel, emit_h=emit_h),
        grid=(n // tm,),
        in_specs=[_row_spec(tm, d), _row_spec(tm, d),
                  pl.BlockSpec((None, tm, p3.shape[2]), lambda i: (layer, i, 0)),
                  _full_spec(wo.shape), _full_spec(wg.shape), _full_spec(wp.shape),
                  _full_spec((1, d))],
        out_specs=out_specs,
        out_shape=out_shape,
        compiler_params=_cparams(1),
        name="out_ple_mid" if emit_h else "out_ple_last",
    )(og, h, p3, wo, wg, wp, g)


def _rwkv_in(hn, mu, w, w0, w1, w2, a0, a1, a2, k_k, k_a, seq, tm):
    n, d = hn.shape
    prev_spec = pl.BlockSpec((8, d), lambda i: (jnp.maximum(i * (tm // 8) - 1, 0), 0))
    vec = _full_spec((1, d))
    outs = 7
    return pl.pallas_call(
        functools.partial(_rwkv_in_kernel, tiles_per_seq=seq // tm),
        grid=(n // tm,),
        in_specs=[_row_spec(tm, d), prev_spec, _full_spec(mu.shape), _full_spec(w.shape), vec,
                  _full_spec(w1.shape), _full_spec(w2.shape), vec, _full_spec(a1.shape),
                  _full_spec(a2.shape), vec, vec],
        out_specs=[_row_spec(tm, d)] * outs,
        out_shape=[jax.ShapeDtypeStruct((n, d), F32)] * outs,
        compiler_params=_cparams(1),
        name="rwkv_in",
    )(hn, hn, mu, w, w0, w1, w2, a0, a1, a2, k_k, k_a)


def _rwkv_scan(r, k, v, kk, b, ld, z, r_k, gn_g, gn_b, batch, seq, rows):
    n, d = r.shape
    nc = seq // CHUNK
    blk = pl.BlockSpec((rows, CHUNK, d), lambda bi, ci: (bi, ci, 0))
    vec = pl.BlockSpec((1, d), lambda bi, ci: (0, 0))
    seqs = [a.reshape(batch, seq, d) for a in (r, k, v, kk, b, ld, z)]
    out = pl.pallas_call(
        _rwkv_scan_kernel,
        grid=(batch // rows, nc),
        in_specs=[blk] * 7 + [vec] * 3,
        out_specs=blk,
        out_shape=jax.ShapeDtypeStruct((batch, seq, d), BF16),
        scratch_shapes=[pltpu.VMEM((rows, d // GROUP, GROUP, GROUP), F32)],
        compiler_params=_cparams(2),
        name="rwkv_scan",
    )(*seqs, r_k, gn_g, gn_b)
    return out.reshape(n, d)


def kernel(x, p, norm_g, attn_w_in, attn_b_in, attn_sinks, attn_w_out, rwkv_mu, rwkv_w_in,
           rwkv_w0, rwkv_w1, rwkv_w2, rwkv_a0, rwkv_a1, rwkv_a2, rwkv_k_k, rwkv_k_a, rwkv_r_k,
           rwkv_gn_g, rwkv_gn_b, rwkv_w_out, ple_w_proj, ple_w_gate, final_norm_g):
    batch, seq, d = x.shape
    depth = p.shape[0]
    assert depth == 2 and d % GROUP == 0 and seq % WINDOW == 0
    n = batch * seq
    tm = min(1024, seq)
    tm_r = min(512, seq)
    row = lambda a: a.reshape(1, -1).astype(F32)
    bf = lambda a: a.astype(BF16)

    h = x.reshape(n, d)
    p3 = p.reshape(depth, n, p.shape[-1])

    q, k, v, z = _attn_in(h, row(norm_g[0]), bf(attn_w_in[0]), row(attn_b_in[0]), seq, tm)
    og = _swa(q, k, v, z, attn_sinks[0].astype(F32), batch, seq)
    h, hn = _out_ple(og, h, p3, 0,bf(attn_w_out[0]), bf(ple_w_gate[0]), bf(ple_w_proj[0]),
                     row(norm_g[1]), tm, emit_h=True)

    r, k, v, z, kk, b, ld = _rwkv_in(
        hn, rwkv_mu[0].astype(F32), bf(rwkv_w_in[0]), row(rwkv_w0[0]), bf(rwkv_w1[0]),
        bf(rwkv_w2[0]), row(rwkv_a0[0]), bf(rwkv_a1[0]), bf(rwkv_a2[0]), row(rwkv_k_k[0]),
        row(rwkv_k_a[0]), seq, tm_r)
    yg = _rwkv_scan(r, k, v, kk, b, ld, z, row(rwkv_r_k[0]), row(rwkv_gn_g[0]),
                    row(rwkv_gn_b[0]), batch, seq,
                    rows=max(r_ for r_ in range(1, SCAN_ROWS + 1) if batch % r_ == 0))
    (out,) = _out_ple(yg, h, p3, 1,bf(rwkv_w_out[0]), bf(ple_w_gate[1]), bf(ple_w_proj[1]),
                      row(final_norm_g), tm, emit_h=False)
    return out.reshape(batch, seq, d)
```

```python
import functools
import math

import jax
import jax.numpy as jnp
from jax import lax
from jax.experimental import pallas as pl
from jax.experimental.pallas import tpu as pltpu

F32 = jnp.float32
BF16 = jnp.bfloat16

HEAD = 64
N_KV = 4
GQA = 4
WINDOW = 128
ROPE_THETA = 10000.0
MASK_VALUE = -1e30
NORM_EPS = 1e-6
GN_EPS = 64e-5
DECAY_SCALE = math.exp(-0.5)
CHUNK = 64
GROUP = 128
SCAN_ROWS = 8
SWA_BLOCKS = 8
VMEM_LIMIT = 56 * 1024 * 1024


def _cparams(n_axes):
    return pltpu.CompilerParams(
        dimension_semantics=("arbitrary",) * n_axes, vmem_limit_bytes=VMEM_LIMIT)


def _rms(x, g):
    ms = jnp.mean(x * x, axis=-1, keepdims=True)
    return x * lax.rsqrt(ms + NORM_EPS) * g


def _split_bf16(x, terms):
    parts = []
    rem = x
    for _ in range(terms):
        p = rem.astype(BF16)
        parts.append(p)
        rem = rem - p.astype(F32)
    return parts


def _dot(a, b):
    return jnp.dot(a, b, preferred_element_type=F32)


def _dot_nt(a, b):
    return lax.dot_general(a, b, (((1,), (1,)), ((), ())), preferred_element_type=F32)


def _dot_tn(a, b):
    return lax.dot_general(a, b, (((0,), (0,)), ((), ())), preferred_element_type=F32)


def _seg_sum(x, ones_bd):
    hi, lo = _split_bf16(x, 2)
    return _dot(hi, ones_bd) + _dot(lo, ones_bd)


def _block_ones():
    r = lax.broadcasted_iota(jnp.int32, (GROUP, GROUP), 0) >> 6
    c = lax.broadcasted_iota(jnp.int32, (GROUP, GROUP), 1) >> 6
    return r == c


def _attn_in_kernel(x_ref, g_ref, w_ref, b_ref, cos_ref, sin_ref, q_ref, k_ref, v_ref, z_ref):
    hn = _rms(x_ref[...], g_ref[...]).astype(BF16)
    cos = cos_ref[...]
    sin = sin_ref[...]
    first = (lax.broadcasted_iota(jnp.int32, cos.shape, 1) & (HEAD - 1)) < HEAD // 2

    def rope(t):
        rot = jnp.where(first, pltpu.roll(t, 128 - HEAD // 2, 1), pltpu.roll(t, HEAD // 2, 1))
        return t * cos + rot * sin

    d = q_ref.shape[1]
    kv = k_ref.shape[1]
    q = _dot(hn, w_ref[:, 0:d]) + b_ref[:, 0:d]
    for c in range(d // 128):
        sl = slice(c * 128, (c + 1) * 128)
        q_ref[:, sl] = (rope(q[:, sl]) * (HEAD ** -0.5)).astype(BF16)
    k = _dot(hn, w_ref[:, d:d + kv]) + b_ref[:, d:d + kv]
    for c in range(kv // 128):
        sl = slice(c * 128, (c + 1) * 128)
        k_ref[:, sl] = rope(k[:, sl]).astype(BF16)
    v = _dot(hn, w_ref[:, d + kv:d + 2 * kv]) + b_ref[:, d + kv:d + 2 * kv]
    v_ref[...] = v.astype(BF16)
    z = _dot(hn, w_ref[:, d + 2 * kv:]) + b_ref[:, d + 2 * kv:]
    z_ref[...] = z * jax.nn.sigmoid(z)


def _swa_kernel(sink_ref, q_ref, kc_ref, kp_ref, vc_ref, vp_ref, zg_ref, h_ref, p_ref, wo_ref,
                wg_ref, wp_ref, g_ref, h_out, hn_out, o_ref):
    n = pl.program_id(1)
    blocks = q_ref.shape[0] // WINDOW
    ki = lax.broadcasted_iota(jnp.int32, (WINDOW, WINDOW), 0)
    qi = lax.broadcasted_iota(jnp.int32, (WINDOW, WINDOW), 1)
    from_prev = qi < ki
    kv_slices = [slice(kh * HEAD, (kh + 1) * HEAD) for kh in range(N_KV)]
    units = [(j, h) for j in range(blocks) for h in range(N_KV * GQA)]

    def window(cur_ref, prev_ref, j, ks):
        if j == 0:
            return jnp.concatenate([prev_ref[:, ks], cur_ref[0:WINDOW, ks]], axis=0)
        return cur_ref[(j - 1) * WINDOW:(j + 1) * WINDOW, ks]

    kcat = [[window(kc_ref, kp_ref, j, ks) for ks in kv_slices] for j in range(blocks)]
    vcat_t = [[window(vc_ref, vp_ref, j, ks).T for ks in kv_slices] for j in range(blocks)]
    s = [_dot_nt(kcat[j][h // GQA], q_ref[j * WINDOW:(j + 1) * WINDOW, h * HEAD:(h + 1) * HEAD])
         for j, h in units]

    def fold(j, x):
        prev = x[:WINDOW]
        if j == 0:
            prev = jnp.where(n > 0, prev, MASK_VALUE)
        return jnp.where(from_prev, prev, x[WINDOW:])

    s = [fold(j, x) for (j, h), x in zip(units, s)]
    m = [jnp.maximum(jnp.max(x, axis=0, keepdims=True), sink_ref[h]) for (j, h), x in zip(units, s)]
    p = [jnp.exp(x - m_) for x, m_ in zip(s, m)]
    l = [jnp.sum(x, axis=0, keepdims=True) + jnp.exp(sink_ref[h] - m_)
         for (j, h), x, m_ in zip(units, p, m)]
    pcat = [jnp.concatenate([jnp.where(from_prev, x, 0.0).astype(BF16),
                             jnp.where(from_prev, 0.0, x).astype(BF16)], axis=0) for x in p]
    o_t = [_dot(vcat_t[j][h // GQA], x) / l_ for (j, h), x, l_ in zip(units, pcat, l)]
    nh = N_KV * GQA
    for j in range(blocks):
        rows = slice(j * WINDOW, (j + 1) * WINDOW)
        o = jnp.concatenate(o_t[j * nh:(j + 1) * nh], axis=0).T
        o_ref[rows, :] = (o * zg_ref[rows, :]).astype(BF16)
    h1 = h_ref[...] + _dot(o_ref[...], wo_ref[...])
    gate = jax.nn.sigmoid(_dot(h1.astype(BF16), wg_ref[...]))
    h2 = h1 + gate * _dot(p_ref[...].astype(BF16), wp_ref[...])
    h_out[...] = h2
    hn_out[...] = _rms(h2, g_ref[...])


def _out_ple_kernel(og_ref, h_ref, p_ref, wo_ref, wg_ref, wp_ref, g_ref, *out_refs, emit_h):
    h1 = h_ref[...] + _dot(og_ref[...], wo_ref[...])
    gate = jax.nn.sigmoid(_dot(h1.astype(BF16), wg_ref[...]))
    h2 = h1 + gate * _dot(p_ref[...].astype(BF16), wp_ref[...])
    if emit_h:
        out_refs[0][...] = h2
    out_refs[-1][...] = _rms(h2, g_ref[...])


def _rwkv_in_kernel(hn_ref, prev_ref, mu_ref, w_ref, w0_ref, w1_ref, w2_ref, a0_ref, a1_ref,
                    a2_ref, kk_ref, ka_ref, r_out, k_out, v_out, z_out, kk_out, b_out, ld_out,
                    *, tiles_per_seq):
    i = pl.program_id(0)
    hn = hn_ref[...]
    tm, d = hn.shape
    prev_row = jnp.where(i % tiles_per_seq == 0, 0.0, prev_ref[7:8, :])
    row = lax.broadcasted_iota(jnp.int32, (tm, d), 0)
    shifted = jnp.where(row == 0, prev_row, pltpu.roll(hn, 1, 0))
    xx = shifted - hn

    def lerp(c):
        return (hn + xx * mu_ref[c:c + 1, :]).astype(BF16)

    r_out[...] = _dot(lerp(0), w_ref[:, 0:d])
    k = _dot(lerp(1), w_ref[:, d:2 * d])
    v_out[...] = _dot(lerp(2), w_ref[:, 2 * d:3 * d])
    z_out[...] = _dot(lerp(3), w_ref[:, 3 * d:4 * d])

    lw = jnp.tanh(_dot(lerp(4), w1_ref[...])).astype(BF16)
    ld_out[...] = -DECAY_SCALE * jax.nn.sigmoid(w0_ref[...] + _dot(lw, w2_ref[...]))

    la = _dot(lerp(5), a1_ref[...]).astype(BF16)
    a = jax.nn.sigmoid(a0_ref[...] + _dot(la, a2_ref[...]))

    ones_bd = jnp.where(_block_ones(), 1.0, 0.0).astype(BF16)
    kk = k * kk_ref[...]
    for g in range(d // GROUP):
        sl = slice(g * GROUP, (g + 1) * GROUP)
        kg = kk[:, sl]
        nrm = jnp.sqrt(_seg_sum(kg * kg, ones_bd))
        kg = kg / jnp.maximum(nrm, 1e-12)
        kk_out[:, sl] = kg
        b_out[:, sl] = kg * a[:, sl]
    k_out[...] = k * (1.0 + (a - 1.0) * ka_ref[...])


def _scan_masks():
    t = lax.broadcasted_iota(jnp.int32, (CHUNK, GROUP), 0)
    s = lax.broadcasted_iota(jnp.int32, (CHUNK, GROUP), 1) & (HEAD - 1)
    same = lambda sh: (t >> sh) == (s >> sh)
    return dict(strict=s < t, incl=s <= t, eye=s == t, b2=same(1), b4=same(2), b8=same(3),
                b16=same(4), b32=same(5))


def _rwkv_scan_kernel(r_ref, k_ref, v_ref, kk_ref, b_ref, ld_ref, z_ref, rk_ref, gg_ref, gb_ref,
                      o_ref, s_ref):
    @pl.when(pl.program_id(1) == 0)
    def _():
        s_ref[...] = jnp.zeros_like(s_ref)

    rows, _, d = r_ref.shape
    lanes = lambda g: slice(g * GROUP, (g + 1) * GROUP)
    bdmask = _block_ones()
    ones_bd = jnp.where(bdmask, 1.0, 0.0).astype(BF16)
    mk = _scan_masks()

    def bd(x):
        return jnp.where(bdmask, jnp.tile(x.astype(BF16), (GROUP // HEAD, 1)), 0)

    def pmm(a_list, b_list):
        return [_dot(a.astype(BF16), bd(b)) for a, b in zip(a_list, b_list)]

    def add(a_list, b_list):
        return [a + b for a, b in zip(a_list, b_list)]

    def masked(name, x_list):
        return [jnp.where(mk[name], x, 0.0) for x in x_list]

    ti = lax.broadcasted_iota(jnp.int32, (CHUNK, 3 * CHUNK), 0)
    tj = lax.broadcasted_iota(jnp.int32, (CHUNK, 3 * CHUNK), 1) & (CHUNK - 1)
    tri3 = jnp.where(tj <= ti, 1.0, 0.0).astype(BF16)

    names = ("r", "k", "v", "kk", "b", "ld", "z")
    refs = (r_ref, k_ref, v_ref, kk_ref, b_ref, ld_ref, z_ref)
    inp = [{nm_: ref[i] for nm_, ref in zip(names, refs)} for i in range(rows)]
    state = {(i, g): s_ref[i, g] for i in range(rows) for g in range(d // GROUP)}
    rk, gg, gb = rk_ref[...], gg_ref[...], gb_ref[...]

    units = [(i, g) for i in range(rows) for g in range(d // GROUP)]
    each = lambda f: [f(i, g) for i, g in units]
    pre = []
    for i in range(rows):
        ld = inp[i]["ld"]
        c = _dot(tri3, jnp.concatenate(_split_bf16(ld, 3), axis=0))
        c_last = c[CHUNK - 1:CHUNK, :]
        r, k, b = inp[i]["r"], inp[i]["k"], inp[i]["b"]
        p_inv = jnp.exp(-c)
        p_rest = jnp.exp(c_last - c)
        pre.append(dict(
            r=r, k=k, v=inp[i]["v"], rt=r * jnp.exp(c), at=-inp[i]["kk"] * jnp.exp(c - ld),
            kt=k * p_inv, bt=b * p_inv, kh=k * p_rest, bh=b * p_rest, p_all=jnp.exp(c_last)))
    get = lambda name: each(lambda i, g: pre[i][name][:, lanes(g)])

    s0 = each(lambda i, g: state[i, g])
    x = each(lambda i, g: jnp.concatenate(
        [pre[i]["at"][:, lanes(g)], pre[i]["rt"][:, lanes(g)]], axis=0).astype(BF16))
    abk = [_dot_nt(xi, jnp.concatenate([bd(bt_), bd(kt_), s_.astype(BF16)], axis=0))
           for xi, bt_, kt_, s_ in zip(x, get("bt"), get("kt"), s0)]
    ab = [m[:, :GROUP] for m in abk]
    ak = [m[:, GROUP:2 * GROUP] for m in abk]
    gs = [m[:, 2 * GROUP:] for m in abk]

    nm = masked("strict", [m[:CHUNK] for m in ab])
    eye = jnp.where(mk["eye"], 1.0, 0.0)
    levels = (("eye", "b2"), ("b2", "b4"), ("b4", "b8"), ("b8", "b16"), ("b16", "b32"),
              ("b32", None))
    q, xn = None, nm
    for lo, hi in levels:
        link = ~mk[lo] if hi is None else (mk[hi] & ~mk[lo])
        off = [jnp.where(link, x_, 0.0).astype(BF16) for x_ in xn]
        if q is None:
            q = [eye + o_ for o_ in off]
            xn = add(xn, pmm(off, xn))
        elif hi is None:
            q = add(q, pmm(off, q))
        else:
            both = [_dot(o_, jnp.concatenate([bd(q_), bd(x_)], axis=1))
                    for o_, q_, x_ in zip(off, q, xn)]
            q = [q_ + m[:, :GROUP] for q_, m in zip(q, both)]
            xn = [x_ + m[:, GROUP:] for x_, m in zip(xn, both)]

    a_rb = masked("incl", [m[CHUNK:] for m in ab])
    a_k = [jnp.concatenate([jnp.where(mk["strict"], m[:CHUNK], 0.0),
                            jnp.where(mk["incl"], m[CHUNK:], 0.0)], axis=0) for m in ak]
    vg = get("v")
    av = pmm(a_k, vg)
    w = [g_[:CHUNK] + a_[:CHUNK] for g_, a_ in zip(gs, av)]
    u = pmm(q, w)
    y = [g_[CHUNK:] + a_[CHUNK:] + c_ for g_, a_, c_ in zip(gs, av, pmm(a_rb, u))]
    ds = [_dot_tn(jnp.concatenate([v_, u_], axis=0).astype(BF16),
                  jnp.concatenate([kh_, bh_], axis=0).astype(BF16))
          for v_, u_, kh_, bh_ in zip(vg, u, get("kh"), get("bh"))]
    new_state = [jnp.where(bdmask, s_ * pre[i]["p_all"][:, lanes(g)] + ds_, 0.0)
                 for (i, g), s_, ds_ in zip(units, s0, ds)]

    rkr = each(lambda i, g: pre[i]["r"][:, lanes(g)] * pre[i]["k"][:, lanes(g)] * rk[:, lanes(g)])
    st1 = [_dot(jnp.concatenate(_split_bf16(y_, 2) + _split_bf16(q_, 2), axis=0), ones_bd)
           for y_, q_ in zip(y, rkr)]
    mean = [(s_[:CHUNK] + s_[CHUNK:2 * CHUNK]) * (1.0 / HEAD) for s_ in st1]
    yc = [y_ - m_ for y_, m_ in zip(y, mean)]
    st2 = [_dot(jnp.concatenate(_split_bf16(c_ * c_, 2), axis=0), ones_bd) for c_ in yc]
    outs = []
    for (i, g), yc_, s1, s2, v_ in zip(units, yc, st1, st2, vg):
        var = (s2[:CHUNK] + s2[CHUNK:]) * (1.0 / HEAD)
        yn = yc_ * lax.rsqrt(var + GN_EPS) * gg[:, lanes(g)] + gb[:, lanes(g)]
        bonus = (s1[2 * CHUNK:3 * CHUNK] + s1[3 * CHUNK:]) * v_
        z = inp[i]["z"][:, lanes(g)]
        outs.append(((yn + bonus) * (z * jax.nn.sigmoid(z))).astype(BF16))
    for (i, g), s_, o_ in zip(units, new_state, outs):
        s_ref[i, g] = s_
        o_ref[i, :, lanes(g)] = o_


def _row_spec(tm, width):
    return pl.BlockSpec((tm, width), lambda i: (i, 0))


def _full_spec(shape):
    return pl.BlockSpec(shape, lambda *_: (0,) * len(shape))


def _rope_tables(seq):
    half = HEAD // 2
    inv = ROPE_THETA ** (-jnp.arange(half, dtype=F32) / half)
    ang = jnp.arange(seq).astype(F32)[:, None] * inv[None, :]
    cos = jnp.cos(ang)
    sin = jnp.sin(ang)
    cos = jnp.tile(jnp.concatenate([cos, cos], axis=-1), (1, 128 // HEAD))
    sin = jnp.tile(jnp.concatenate([-sin, sin], axis=-1), (1, 128 // HEAD))
    return cos, sin


def _attn_in(x2, g, w, bias, seq, tm):
    n, d = x2.shape
    kv = N_KV * HEAD
    cos, sin = _rope_tables(seq)
    per_seq = seq // tm
    tab_spec = pl.BlockSpec((tm, 128), lambda i: (i % per_seq, 0))
    return pl.pallas_call(
        _attn_in_kernel,
        grid=(n // tm,),
        in_specs=[_row_spec(tm, d), _full_spec((1, d)), _full_spec(w.shape),
                  _full_spec((1, w.shape[1])), tab_spec, tab_spec],
        out_specs=[_row_spec(tm, d), _row_spec(tm, kv), _row_spec(tm, kv), _row_spec(tm, d)],
        out_shape=[jax.ShapeDtypeStruct((n, d), BF16), jax.ShapeDtypeStruct((n, kv), BF16),
                   jax.ShapeDtypeStruct((n, kv), BF16), jax.ShapeDtypeStruct((n, d), F32)],
        compiler_params=_cparams(1),
        name="attn_in",
    )(x2, g, w, bias, cos, sin)


def _swa(q, k, v, z, sinks, h, p3, wo, wg, wp, g, batch, seq):
    n, d = q.shape
    kv = k.shape[1]
    blocks = max(b_ for b_ in range(1, SWA_BLOCKS + 1) if (seq // WINDOW) % b_ == 0)
    nb = seq // (blocks * WINDOW)
    cur = lambda width: pl.BlockSpec((blocks * WINDOW, width), lambda b, j: (b * nb + j, 0))
    prev = lambda width: pl.BlockSpec(
        (WINDOW, width), lambda b, j: ((b * nb + j) * blocks - jnp.minimum(j, 1), 0))
    return pl.pallas_call(
        _swa_kernel,
        grid=(batch, nb),
        in_specs=[pl.BlockSpec(memory_space=pltpu.SMEM), cur(d), cur(kv), prev(kv), cur(kv),
                  prev(kv), cur(d), cur(d),
                  pl.BlockSpec((None, blocks * WINDOW, p3.shape[2]), lambda b, j: (0, b * nb + j, 0)),
                  _full_spec(wo.shape), _full_spec(wg.shape), _full_spec(wp.shape),
                  _full_spec((1, d))],
        out_specs=[cur(d), cur(d)],
        out_shape=[jax.ShapeDtypeStruct((n, d), F32)] * 2,
        scratch_shapes=[pltpu.VMEM((blocks * WINDOW, d), BF16)],
        compiler_params=_cparams(2),
        name="swa_out_ple",
    )(sinks, q, k, k, v, v, z, h, p3, wo, wg, wp, g)


def _out_ple(og, h, p3, layer, wo, wg, wp, g, tm, emit_h):
    n, d = h.shape
    out_shape = [jax.ShapeDtypeStruct((n, d), F32)] * (2 if emit_h else 1)
    out_specs = [_row_spec(tm, d)] * (2 if emit_h else 1)
    return pl.pallas_call(
        functools.partial(_out_ple_kernel, emit_h=emit_h),
        grid=(n // tm,),
        in_specs=[_row_spec(tm, d), _row_spec(tm, d),
                  pl.BlockSpec((None, tm, p3.shape[2]), lambda i: (layer, i, 0)),
                  _full_spec(wo.shape), _full_spec(wg.shape), _full_spec(wp.shape),
                  _full_spec((1, d))],
        out_specs=out_specs,
        out_shape=out_shape,
        compiler_params=_cparams(1),
        name="out_ple_mid" if emit_h else "out_ple_last",
    )(og, h, p3, wo, wg, wp, g)


def _rwkv_in(hn, mu, w, w0, w1, w2, a0, a1, a2, k_k, k_a, seq, tm):
    n, d = hn.shape
    prev_spec = pl.BlockSpec((8, d), lambda i: (jnp.maximum(i * (tm // 8) - 1, 0), 0))
    vec = _full_spec((1, d))
    outs = 7
    return pl.pallas_call(
        functools.partial(_rwkv_in_kernel, tiles_per_seq=seq // tm),
        grid=(n // tm,),
        in_specs=[_row_spec(tm, d), prev_spec, _full_spec(mu.shape), _full_spec(w.shape), vec,
                  _full_spec(w1.shape), _full_spec(w2.shape), vec, _full_spec(a1.shape),
                  _full_spec(a2.shape), vec, vec],
        out_specs=[_row_spec(tm, d)] * outs,
        out_shape=[jax.ShapeDtypeStruct((n, d), F32)] * outs,
        compiler_params=_cparams(1),
        name="rwkv_in",
    )(hn, hn, mu, w, w0, w1, w2, a0, a1, a2, k_k, k_a)


def _rwkv_scan(r, k, v, kk, b, ld, z, r_k, gn_g, gn_b, batch, seq, rows):
    n, d = r.shape
    nc = seq // CHUNK
    blk = pl.BlockSpec((rows, CHUNK, d), lambda bi, ci: (bi, ci, 0))
    vec = pl.BlockSpec((1, d), lambda bi, ci: (0, 0))
    seqs = [a.reshape(batch, seq, d) for a in (r, k, v, kk, b, ld, z)]
    out = pl.pallas_call(
        _rwkv_scan_kernel,
        grid=(batch // rows, nc),
        in_specs=[blk] * 7 + [vec] * 3,
        out_specs=blk,
        out_shape=jax.ShapeDtypeStruct((batch, seq, d), BF16),
        scratch_shapes=[pltpu.VMEM((rows, d // GROUP, GROUP, GROUP), F32)],
        compiler_params=_cparams(2),
        name="rwkv_scan",
    )(*seqs, r_k, gn_g, gn_b)
    return out.reshape(n, d)


def kernel(x, p, norm_g, attn_w_in, attn_b_in, attn_sinks, attn_w_out, rwkv_mu, rwkv_w_in,
           rwkv_w0, rwkv_w1, rwkv_w2, rwkv_a0, rwkv_a1, rwkv_a2, rwkv_k_k, rwkv_k_a, rwkv_r_k,
           rwkv_gn_g, rwkv_gn_b, rwkv_w_out, ple_w_proj, ple_w_gate, final_norm_g):
    batch, seq, d = x.shape
    depth = p.shape[0]
    assert depth == 2 and d % GROUP == 0 and seq % WINDOW == 0
    n = batch * seq
    tm = min(1024, seq)
    tm_r = min(512, seq)
    row = lambda a: a.reshape(1, -1).astype(F32)
    bf = lambda a: a.astype(BF16)

    h = x.reshape(n, d)
    p3 = p.reshape(depth, n, p.shape[-1])

    q, k, v, z = _attn_in(h, row(norm_g[0]), bf(attn_w_in[0]), row(attn_b_in[0]), seq, tm)
    h, hn = _swa(q, k, v, z, attn_sinks[0].astype(F32), h, p3, bf(attn_w_out[0]),
                 bf(ple_w_gate[0]), bf(ple_w_proj[0]), row(norm_g[1]), batch, seq)

    r, k, v, z, kk, b, ld = _rwkv_in(
        hn, rwkv_mu[0].astype(F32), bf(rwkv_w_in[0]), row(rwkv_w0[0]), bf(rwkv_w1[0]),
        bf(rwkv_w2[0]), row(rwkv_a0[0]), bf(rwkv_a1[0]), bf(rwkv_a2[0]), row(rwkv_k_k[0]),
        row(rwkv_k_a[0]), seq, tm_r)
    yg = _rwkv_scan(r, k, v, kk, b, ld, z, row(rwkv_r_k[0]), row(rwkv_gn_g[0]),
                    row(rwkv_gn_b[0]), batch, seq,
                    rows=max(r_ for r_ in range(1, SCAN_ROWS + 1) if batch % r_ == 0))
    (out,) = _out_ple(yg, h, p3, 1,bf(rwkv_w_out[0]), bf(ple_w_gate[1]), bf(ple_w_proj[1]),
                      row(final_norm_g), tm, emit_h=False)
    return out.reshape(batch, seq, d)
```
